```python
import math
import jax, jax.numpy as jnp
from jax import lax
import numpy as np

D_MODEL = 1024
BATCH = 32
SEQ = 2048
DEPTH = 1

PLE_DIM = 256
D_FF = 2816
POOL_WIDTH = 512
POOL_WINDOWS = (2, 4, 8, 16)
POOL_GROUPS = len(POOL_WINDOWS)
POOL_GROUP_DIM = POOL_WIDTH // POOL_GROUPS
HEAD_DIM = 64
N_Q_HEADS = 8
N_KV_HEADS = 2
Q_PER_KV = N_Q_HEADS // N_KV_HEADS
ATTN_WIDTH = N_Q_HEADS * HEAD_DIM
KV_WIDTH = N_KV_HEADS * HEAD_DIM
WINDOW = 128
BLOCK = 128
MIX_IN = POOL_WIDTH + ATTN_WIDTH + 2 * KV_WIDTH
MIX_OUT = POOL_WIDTH + ATTN_WIDTH
EPS = 1e-6
NEG_INF = -1e30

kernel_name = "hymba_pool_swa_sink_macaron_ple"


def rms_norm(x, g):
    xf = x.astype(jnp.float32)
    y = xf * lax.rsqrt(jnp.mean(xf * xf, axis=-1, keepdims=True) + EPS)
    return (y * g.astype(jnp.float32)).astype(x.dtype)


def swiglu(h, w_gate, w_up, w_down):
    return (jax.nn.silu(h @ w_gate) * (h @ w_up)) @ w_down


def causal_pool_mixer(u, pool_w, pool_scale):
    B, S, _ = u.shape
    ug_all = u.reshape(B, S, POOL_GROUPS, POOL_GROUP_DIM)
    pos = jnp.arange(S)
    outs = []
    for g, win in enumerate(POOL_WINDOWS):
        ug = ug_all[:, :, g, :].astype(jnp.float32)
        c = jnp.cumsum(ug, axis=1)
        c_shift = jnp.pad(c, ((0, 0), (win, 0), (0, 0)))[:, :S]
        count = jnp.minimum(pos + 1, win).astype(jnp.float32)[None, :, None]
        outs.append(((c - c_shift) / count - ug).astype(u.dtype))
    d = jnp.stack(outs, axis=2)
    y = jnp.einsum('bsgc,gcd->bsgd', d, pool_w).reshape(B, S, POOL_WIDTH)
    return y * pool_scale


def sliding_window_sink_attention(q, k, v, sinks):
    B, S, _, _ = q.shape
    nb = S // BLOCK
    qb = q.reshape(B, nb, BLOCK, N_KV_HEADS, Q_PER_KV, HEAD_DIM)
    kb = k.reshape(B, nb, BLOCK, N_KV_HEADS, HEAD_DIM)
    vb = v.reshape(B, nb, BLOCK, N_KV_HEADS, HEAD_DIM)
    pad = ((0, 0), (1, 0), (0, 0), (0, 0), (0, 0))
    k_band = jnp.concatenate([jnp.pad(kb, pad)[:, :nb], kb], axis=2)
    v_band = jnp.concatenate([jnp.pad(vb, pad)[:, :nb], vb], axis=2)
    scale = 1.0 / math.sqrt(HEAD_DIM)
    scores = jnp.einsum('bnqkgd,bnskd->bnkgqs', qb, k_band).astype(jnp.float32) * scale
    i = jnp.arange(BLOCK)[:, None]
    j = jnp.arange(2 * BLOCK)[None, :]
    rel = i + BLOCK - j
    band = (rel >= 0) & (rel < WINDOW)
    has_prev = (jnp.arange(nb)[:, None, None] > 0) | (j >= BLOCK)[None]
    mask = band[None] & has_prev
    scores = jnp.where(mask[None, :, None, None], scores, NEG_INF)
    s = sinks.astype(jnp.float32).reshape(N_KV_HEADS, Q_PER_KV)[None, None, :, :, None, None]
    m = jnp.maximum(jnp.max(scores, axis=-1, keepdims=True), s)
    e = jnp.exp(scores - m)
    probs = e / (jnp.sum(e, axis=-1, keepdims=True) + jnp.exp(s - m))
    out = jnp.einsum('bnkgqs,bnskd->bnqkgd', probs.astype(v.dtype), v_band)
    return out.reshape(B, S, ATTN_WIDTH)


def setup_inputs(seed: int = 0) -> dict:
    key = jax.random.key(seed)
    ks = jax.random.split(key, 24)
    f32 = jnp.float32

    def w(k, shape, fan_in):
        return jax.random.normal(k, shape, f32) * (fan_in ** -0.5)

    def gain(k, shape):
        return 1.0 + 0.05 * jax.random.normal(k, shape, f32)

    L = DEPTH
    return {
        "x": jax.random.normal(ks[0], (BATCH, SEQ, D_MODEL), f32),
        "p": jax.random.normal(ks[1], (DEPTH, BATCH, SEQ, PLE_DIM), f32),
        "ffn1_norm": gain(ks[2], (L, D_MODEL)),
        "ffn1_w_gate": w(ks[3], (L, D_MODEL, D_FF), D_MODEL),
        "ffn1_w_up": w(ks[4], (L, D_MODEL, D_FF), D_MODEL),
        "ffn1_w_down": w(ks[5], (L, D_FF, D_MODEL), D_FF),
        "mix_norm": gain(ks[6], (L, D_MODEL)),
        "w_in": w(ks[7], (L, D_MODEL, MIX_IN), D_MODEL),
        "pool_w": w(ks[8], (L, POOL_GROUPS, POOL_GROUP_DIM, POOL_GROUP_DIM), POOL_GROUP_DIM),
        "pool_scale": gain(ks[9], (L, POOL_WIDTH)),
        "q_norm": gain(ks[10], (L, HEAD_DIM)),
        "k_norm": gain(ks[11], (L, HEAD_DIM)),
        "attn_sinks": jax.random.normal(ks[12], (L, N_Q_HEADS), f32),
        "pool_out_norm": gain(ks[13], (L, POOL_WIDTH)),
        "attn_out_norm": gain(ks[14], (L, ATTN_WIDTH)),
        "w_out": w(ks[15], (L, MIX_OUT, D_MODEL), MIX_OUT),
        "ffn2_norm": gain(ks[16], (L, D_MODEL)),
        "ffn2_w_gate": w(ks[17], (L, D_MODEL, D_FF), D_MODEL),
        "ffn2_w_up": w(ks[18], (L, D_MODEL, D_FF), D_MODEL),
        "ffn2_w_down": w(ks[19], (L, D_FF, D_MODEL), D_FF),
        "ple_norm": gain(ks[20], (L, D_MODEL)),
        "w_ple_gate": w(ks[21], (L, D_MODEL, D_MODEL), D_MODEL),
        "w_ple_proj": w(ks[22], (L, PLE_DIM, D_MODEL), PLE_DIM),
    }


def reference(x, p, ffn1_norm, ffn1_w_gate, ffn1_w_up, ffn1_w_down, mix_norm, w_in,
              pool_w, pool_scale, q_norm, k_norm, attn_sinks, pool_out_norm, attn_out_norm,
              w_out, ffn2_norm, ffn2_w_gate, ffn2_w_up, ffn2_w_down, ple_norm,
              w_ple_gate, w_ple_proj):
    B, S, _ = x.shape
    o_q = POOL_WIDTH
    o_k = o_q + ATTN_WIDTH
    o_v = o_k + KV_WIDTH
    for i in range(DEPTH):
        h = rms_norm(x, ffn1_norm[i])
        x = x + 0.5 * swiglu(h, ffn1_w_gate[i], ffn1_w_up[i], ffn1_w_down[i])

        h = rms_norm(x, mix_norm[i])
        z = h @ w_in[i]
        u = z[..., :o_q]
        q = z[..., o_q:o_k].reshape(B, S, N_Q_HEADS, HEAD_DIM)
        k = z[..., o_k:o_v].reshape(B, S, N_KV_HEADS, HEAD_DIM)
        v = z[..., o_v:].reshape(B, S, N_KV_HEADS, HEAD_DIM)

        pool_y = causal_pool_mixer(u, pool_w[i], pool_scale[i])
        q = rms_norm(q, q_norm[i])
        k = rms_norm(k, k_norm[i])
        attn_y = sliding_window_sink_attention(q, k, v, attn_sinks[i])

        merged = jnp.concatenate([rms_norm(pool_y, pool_out_norm[i]),
                                  rms_norm(attn_y, attn_out_norm[i])], axis=-1)
        x = x + merged @ w_out[i]

        h = rms_norm(x, ffn2_norm[i])
        x = x + 0.5 * swiglu(h, ffn2_w_gate[i], ffn2_w_up[i], ffn2_w_down[i])

        gate = jax.nn.sigmoid(rms_norm(x, ple_norm[i]) @ w_ple_gate[i])
        x = x + gate * (p[i] @ w_ple_proj[i])
    return x
```

```python
import functools
import math

import jax
import jax.numpy as jnp
from jax import lax
from jax.experimental import pallas as pl
from jax.experimental.pallas import tpu as pltpu

D_MODEL = 1024
PLE_DIM = 256
D_FF = 2816
POOL_WIDTH = 512
POOL_WINDOWS = (2, 4, 8, 16)
POOL_GROUP_DIM = 128
HEAD_DIM = 64
N_Q_HEADS = 8
N_KV_HEADS = 2
ATTN_WIDTH = N_Q_HEADS * HEAD_DIM
KV_WIDTH = N_KV_HEADS * HEAD_DIM
WINDOW = 128
BLOCK = 128
MIX_IN = POOL_WIDTH + ATTN_WIDTH + 2 * KV_WIDTH
EPS = 1e-6
NEG_INF = -1e30

LANES = 128
MXU_DIM = 256
POOL_HALO = 16
TOKEN_TILE = 512
FF_CHUNK = 2 * MXU_DIM
VMEM_LIMIT_BYTES = 56 * 1024 * 1024

F32 = jnp.float32
BF16 = jnp.bfloat16


def _ff_chunks():
    chunks, lo = [], 0
    while lo < D_FF:
        sz = min(FF_CHUNK, D_FF - lo)
        chunks.append((lo, sz))
        lo += sz
    return chunks


def _rms(x, g):
    return x * lax.rsqrt(jnp.mean(x * x, axis=-1, keepdims=True) + EPS) * g


def _ffn_half_step(x, g, wg_ref, wu_ref, wd_ref):
    h = _rms(x, g).astype(BF16)
    acc = None
    for lo, sz in _ff_chunks():
        gate = jnp.dot(h, wg_ref[:, lo:lo + sz], preferred_element_type=F32)
        up = jnp.dot(h, wu_ref[:, lo:lo + sz], preferred_element_type=F32)
        act = (gate * jax.nn.sigmoid(gate) * up).astype(BF16)
        part = jnp.dot(act, wd_ref[lo:lo + sz, :], preferred_element_type=F32)
        acc = part if acc is None else acc + part
    return x + 0.5 * acc


def _head_pair_norm(zb, gain2, low):
    sq = zb * zb
    ms_lo = jnp.sum(jnp.where(low, sq, 0.0), axis=-1, keepdims=True) * (1.0 / HEAD_DIM)
    ms_hi = jnp.sum(jnp.where(low, 0.0, sq), axis=-1, keepdims=True) * (1.0 / HEAD_DIM)
    r = jnp.where(low, lax.rsqrt(ms_lo + EPS), lax.rsqrt(ms_hi + EPS))
    return zb * r * gain2


def _dup_halves(a, low):
    rot = pltpu.roll(a, HEAD_DIM, axis=1)
    return jnp.where(low, a, rot), jnp.where(low, rot, a)


def _stage1_kernel(x_ref, n1_ref, wg_ref, wu_ref, wd_ref, nm_ref, win_ref, qg_ref, kg_ref,
                   x1_ref, u_ref, q_ref, kv_ref):
    x1 = _ffn_half_step(x_ref[...], n1_ref[...], wg_ref, wu_ref, wd_ref)
    x1_ref[...] = x1
    h = _rms(x1, nm_ref[...]).astype(BF16)
    z = jnp.dot(h, win_ref[...], preferred_element_type=F32)
    u_ref[...] = z[:, :POOL_WIDTH]
    low = lax.broadcasted_iota(jnp.int32, (1, LANES), 1) < HEAD_DIM
    scale = 1.0 / math.sqrt(HEAD_DIM)
    qg = qg_ref[...]
    for j in range(ATTN_WIDTH // LANES):
        lo = POOL_WIDTH + j * LANES
        qn = _head_pair_norm(z[:, lo:lo + LANES], qg, low) * scale
        q_ref[:, j * LANES:(j + 1) * LANES] = qn.astype(BF16)
    o_k = POOL_WIDTH + ATTN_WIDTH
    kn = _head_pair_norm(z[:, o_k:o_k + KV_WIDTH], kg_ref[...], low)
    k0, k1 = _dup_halves(kn, low)
    v0, v1 = _dup_halves(z[:, o_k + KV_WIDTH:o_k + 2 * KV_WIDTH], low)
    for j, a in enumerate((k0, k1, v0, v1)):
        kv_ref[:, j * LANES:(j + 1) * LANES] = a.astype(BF16)


def _stage2_kernel(sinks_ref, x1_ref, u_ref, uh_ref, q_ref, kv_ref, kvh_ref, p_ref,
                   pw_ref, ps_ref, pon_ref, aon_ref, wout_ref,
                   n2_ref, wg_ref, wu_ref, wd_ref, pn_ref, wpg_ref, wpp_ref,
                   out_ref, uext_ref, merged_ref, attn_ref, *, tiles_per_seq):
    tile = x1_ref.shape[0]
    i = pl.program_id(0)
    seq_tile = lax.rem(i, tiles_per_seq)
    first = seq_tile == 0

    uext_ref[0:POOL_HALO, :] = jnp.where(first, 0.0, uh_ref[...])
    uext_ref[POOL_HALO:, :] = u_ref[...]
    pos1 = seq_tile * tile + lax.broadcasted_iota(jnp.int32, (tile, 1), 0) + 1
    for g, win in enumerate(POOL_WINDOWS):
        cols = slice(g * POOL_GROUP_DIM, (g + 1) * POOL_GROUP_DIM)
        tok = uext_ref[POOL_HALO:, cols]
        s = tok
        for k in range(1, win):
            s = s + uext_ref[POOL_HALO - k:POOL_HALO - k + tile, cols]
        inv_cnt = 1.0 / jnp.minimum(pos1, win).astype(F32)
        d = (s * inv_cnt - tok).astype(BF16)
        y = jnp.dot(d, pw_ref[g], preferred_element_type=F32)
        attn_ref[:, cols] = y * ps_ref[:, cols]
    pool_n = _rms(attn_ref[...], pon_ref[...])
    merged_ref[:, :POOL_WIDTH] = pool_n.astype(BF16)

    low = lax.broadcasted_iota(jnp.int32, (1, LANES), 1) < HEAD_DIM
    rows = lax.broadcasted_iota(jnp.int32, (2 * BLOCK, 2 * BLOCK), 0)
    keys = lax.broadcasted_iota(jnp.int32, (2 * BLOCK, 2 * BLOCK), 1)
    qpos = jnp.where(rows >= BLOCK, rows - BLOCK, rows)
    band = (keys > qpos) & (keys <= qpos + BLOCK)
    band_first = band & (keys >= jnp.where(first, BLOCK, 0))
    top =lax.broadcasted_iota(jnp.int32, (2 * BLOCK, 1), 0) < BLOCK
    zero_bf = jnp.zeros((), BF16)
    for qb in range(tile // BLOCK):
        r0 = qb * BLOCK
        if qb == 0:
            kv_band = jnp.concatenate([kvh_ref[...], kv_ref[0:BLOCK, :]], axis=0)
            mask = band_first
        else:
            kv_band = kv_ref[r0 - BLOCK:r0 + BLOCK, :]
            mask = band
        for j in range(ATTN_WIDTH // LANES):
            kvh = j // (ATTN_WIDTH // LANES // N_KV_HEADS)
            kd = kv_band[:, kvh * LANES:(kvh + 1) * LANES]
            vd = kv_band[:, (N_KV_HEADS + kvh) * LANES:(N_KV_HEADS + kvh + 1) * LANES]
            qblk = q_ref[r0:r0 + BLOCK, j * LANES:(j + 1) * LANES]
            qq = jnp.concatenate([jnp.where(low, qblk, zero_bf), jnp.where(low, zero_bf, qblk)], axis=0)
            s = lax.dot_general(qq, kd, (((1,), (1,)), ((), ())), preferred_element_type=F32)
            s = jnp.where(mask, s, NEG_INF)
            sink = jnp.where(top, sinks_ref[0, 2 * j], sinks_ref[0, 2 * j + 1])
            m = jnp.maximum(jnp.max(s, axis=-1, keepdims=True), sink)
            e = jnp.exp(s - m)
            denom = jnp.sum(e, axis=-1, keepdims=True) + jnp.exp(sink - m)
            o = jnp.dot(e.astype(BF16), vd, preferred_element_type=F32) * (1.0 / denom)
            attn_ref[r0:r0 + BLOCK, j * LANES:(j + 1) * LANES] = jnp.where(low, o[:BLOCK], o[BLOCK:])
    attn_n = _rms(attn_ref[...], aon_ref[...])
    merged_ref[:, POOL_WIDTH:] = attn_n.astype(BF16)

    x2 = x1_ref[...] + jnp.dot(merged_ref[...], wout_ref[...], preferred_element_type=F32)
    x3 = _ffn_half_step(x2, n2_ref[...], wg_ref, wu_ref, wd_ref)
    hg = _rms(x3, pn_ref[...]).astype(BF16)
    gate = jax.nn.sigmoid(jnp.dot(hg, wpg_ref[...], preferred_element_type=F32))
    emb = jnp.dot(p_ref[...].astype(BF16), wpp_ref[...], preferred_element_type=F32)
    out_ref[...] = x3 + gate * emb


def _resident(shape):
    nd = len(shape)
    return pl.BlockSpec(shape, lambda i: (0,) * nd, pipeline_mode=pl.Buffered(1))


def _rows(tile, width):
    return pl.BlockSpec((tile, width), lambda i: (i, 0))


def _stage1(x, n1, wg, wu, wd, nm, win, qg2, kg2):
    n = x.shape[0]
    tile = TOKEN_TILE
    return pl.pallas_call(
        _stage1_kernel,
        grid=(n // tile,),
        in_specs=[_rows(tile, D_MODEL), _resident(n1.shape), _resident(wg.shape), _resident(wu.shape),
                  _resident(wd.shape), _resident(nm.shape), _resident(win.shape),
                  _resident(qg2.shape), _resident(kg2.shape)],
        out_specs=[_rows(tile, D_MODEL), _rows(tile, POOL_WIDTH), _rows(tile, ATTN_WIDTH),
                   _rows(tile, 4 * LANES)],
        out_shape=[jax.ShapeDtypeStruct((n, D_MODEL), F32), jax.ShapeDtypeStruct((n, POOL_WIDTH), F32),
                   jax.ShapeDtypeStruct((n, ATTN_WIDTH), BF16), jax.ShapeDtypeStruct((n, 4 * LANES), BF16)],
        compiler_params=pltpu.CompilerParams(dimension_semantics=("arbitrary",),
                                             vmem_limit_bytes=VMEM_LIMIT_BYTES),
        name="ffn1_mix_in",
    )(x, n1, wg, wu, wd, nm, win, qg2, kg2)


def _stage2(sinks, x1, u, q, kv, p, pw, ps, pon, aon, wout, n2, wg, wu, wd, pn, wpg, wpp, seq):
    n = x1.shape[0]
    tile = TOKEN_TILE
    tiles_per_seq = seq // tile
    halo_u = tile // POOL_HALO
    halo_kv = tile // BLOCK
    res = [pw, ps, pon, aon, wout, n2, wg, wu, wd, pn, wpg, wpp]
    return pl.pallas_call(
        functools.partial(_stage2_kernel, tiles_per_seq=tiles_per_seq),
        grid=(n // tile,),
        in_specs=[pl.BlockSpec(memory_space=pltpu.SMEM),
                  _rows(tile, D_MODEL),
                  _rows(tile, POOL_WIDTH),
                  pl.BlockSpec((POOL_HALO, POOL_WIDTH), lambda i: (jnp.maximum(i * halo_u - 1, 0), 0)),
                  _rows(tile, ATTN_WIDTH),
                  _rows(tile, 4 * LANES),
                  pl.BlockSpec((BLOCK, 4 * LANES), lambda i: (jnp.maximum(i * halo_kv - 1, 0), 0)),
                  _rows(tile, PLE_DIM)] + [_resident(a.shape) for a in res],
        out_specs=_rows(tile, D_MODEL),
        out_shape=jax.ShapeDtypeStruct((n, D_MODEL), F32),
        scratch_shapes=[pltpu.VMEM((tile + POOL_HALO, POOL_WIDTH), F32),
                        pltpu.VMEM((tile, POOL_WIDTH + ATTN_WIDTH), BF16),
                        pltpu.VMEM((tile, ATTN_WIDTH), F32)],
        compiler_params=pltpu.CompilerParams(dimension_semantics=("arbitrary",),
                                             vmem_limit_bytes=VMEM_LIMIT_BYTES),
        name="mix_ffn2_ple",
    )(sinks, x1, u, u, q, kv, kv, p, *res)


def kernel(x, p, ffn1_norm, ffn1_w_gate, ffn1_w_up, ffn1_w_down, mix_norm, w_in, pool_w, pool_scale, q_norm, k_norm, attn_sinks, pool_out_norm, attn_out_norm, w_out, ffn2_norm, ffn2_w_gate, ffn2_w_up, ffn2_w_down, ple_norm, w_ple_gate, w_ple_proj):
    b, s, d = x.shape
    assert d == D_MODEL and s % TOKEN_TILE == 0 and TOKEN_TILE % BLOCK == 0
    depth = p.shape[0]
    xf = x.reshape(b * s, d)
    row = lambda a: a.reshape(1, -1)
    for i in range(depth):
        x1, u, q, kv = _stage1(
            xf, row(ffn1_norm[i]), ffn1_w_gate[i].astype(BF16), ffn1_w_up[i].astype(BF16),
            ffn1_w_down[i].astype(BF16), row(mix_norm[i]), w_in[i].astype(BF16),
            row(jnp.tile(q_norm[i], 2)), row(jnp.tile(k_norm[i], 2)))
        xf = _stage2(
            row(attn_sinks[i]), x1, u, q, kv, p[i].reshape(b * s, PLE_DIM),
            pool_w[i].astype(BF16), row(pool_scale[i]), row(pool_out_norm[i]), row(attn_out_norm[i]),
            w_out[i].astype(BF16), row(ffn2_norm[i]), ffn2_w_gate[i].astype(BF16),
            ffn2_w_up[i].astype(BF16), ffn2_w_down[i].astype(BF16), row(ple_norm[i]),
            w_ple_gate[i].astype(BF16), w_ple_proj[i].astype(BF16), s)
    return xf.reshape(b, s, d)
```

```python
import functools
import math

import jax
import jax.numpy as jnp
from jax import lax
from jax.experimental import pallas as pl
from jax.experimental.pallas import tpu as pltpu

D_MODEL = 1024
PLE_DIM = 256
D_FF = 2816
POOL_WIDTH = 512
POOL_WINDOWS = (2, 4, 8, 16)
POOL_GROUP_DIM = 128
HEAD_DIM = 64
N_Q_HEADS = 8
N_KV_HEADS = 2
ATTN_WIDTH = N_Q_HEADS * HEAD_DIM
KV_WIDTH = N_KV_HEADS * HEAD_DIM
WINDOW = 128
BLOCK = 128
MIX_IN = POOL_WIDTH + ATTN_WIDTH + 2 * KV_WIDTH
EPS = 1e-6
NEG_INF = -1e30

LANES = 128
MXU_DIM = 256
POOL_HALO = 16
TOKEN_TILE = 512
FF_CHUNK = 2 * MXU_DIM
VMEM_LIMIT_BYTES = 56 * 1024 * 1024

F32 = jnp.float32
BF16 = jnp.bfloat16


def _ff_chunks():
    chunks, lo = [], 0
    while lo < D_FF:
        sz = min(FF_CHUNK, D_FF - lo)
        chunks.append((lo, sz))
        lo += sz
    return chunks


def _rms(x, g):
    return x * lax.rsqrt(jnp.mean(x * x, axis=-1, keepdims=True) + EPS) * g


def _ffn_half_step(x, g, wg_ref, wu_ref, wd_ref, between=None):
    between = between or (lambda: None)
    h = _rms(x, g).astype(BF16)
    acc = None
    for lo, sz in _ff_chunks():
        gate = jnp.dot(h, wg_ref[:, lo:lo + sz], preferred_element_type=F32)
        between()
        up = jnp.dot(h, wu_ref[:, lo:lo + sz], preferred_element_type=F32)
        between()
        act = (gate * jax.nn.sigmoid(gate) * up).astype(BF16)
        part = jnp.dot(act, wd_ref[lo:lo + sz, :], preferred_element_type=F32)
        acc = part if acc is None else acc + part
        between()
    return x + 0.5 * acc


def _head_pair_norm(zb, gain2, low):
    sq = zb * zb
    ms_lo = jnp.sum(jnp.where(low, sq, 0.0), axis=-1, keepdims=True) * (1.0 / HEAD_DIM)
    ms_hi = jnp.sum(jnp.where(low, 0.0, sq), axis=-1, keepdims=True) * (1.0 / HEAD_DIM)
    r = jnp.where(low, lax.rsqrt(ms_lo + EPS), lax.rsqrt(ms_hi + EPS))
    return zb * r * gain2


def _dup_halves(a, low):
    rot = pltpu.roll(a, HEAD_DIM, axis=1)
    return jnp.where(low, a, rot), jnp.where(low, rot, a)


def _stage1_kernel(x_ref, n1_ref, wg_ref, wu_ref, wd_ref, nm_ref, win_ref, qg_ref, kg_ref,
                   x1_ref, u_ref, q_ref, kv_ref):
    x1 = _ffn_half_step(x_ref[...], n1_ref[...], wg_ref, wu_ref, wd_ref)
    x1_ref[...] = x1
    h = _rms(x1, nm_ref[...]).astype(BF16)
    z = jnp.dot(h, win_ref[...], preferred_element_type=F32)
    u_ref[...] = z[:, :POOL_WIDTH]
    low = lax.broadcasted_iota(jnp.int32, (1, LANES), 1) < HEAD_DIM
    scale = 1.0 / math.sqrt(HEAD_DIM)
    qg = qg_ref[...]
    for j in range(ATTN_WIDTH // LANES):
        lo = POOL_WIDTH + j * LANES
        qn = _head_pair_norm(z[:, lo:lo + LANES], qg, low) * scale
        q_ref[:, j * LANES:(j + 1) * LANES] = qn.astype(BF16)
    o_k = POOL_WIDTH + ATTN_WIDTH
    kn = _head_pair_norm(z[:, o_k:o_k + KV_WIDTH], kg_ref[...], low)
    k0, k1 = _dup_halves(kn, low)
    v0, v1 = _dup_halves(z[:, o_k + KV_WIDTH:o_k + 2 * KV_WIDTH], low)
    for j, a in enumerate((k0, k1, v0, v1)):
        kv_ref[:, j * LANES:(j + 1) * LANES] = a.astype(BF16)


def _mixer_tasks(first, seq_tile, sinks_ref, u_ref, uh_ref, q_ref, kv_ref, kvh_ref,
                 pw_ref, ps_ref, pon_ref, aon_ref, uext_ref, pool_ref, attn_ref, bias_ref, merged_ref):
    tile = u_ref.shape[0]
    pool_tasks, attn_tasks = [], []

    def pool_group(g, win):
        cols = slice(g * POOL_GROUP_DIM, (g + 1) * POOL_GROUP_DIM)
        if g == 0:
            uext_ref[0:POOL_HALO, :] = jnp.where(first, 0.0, uh_ref[...])
            uext_ref[POOL_HALO:, :] = u_ref[...]
            keys = lax.broadcasted_iota(jnp.int32, (2 * BLOCK, 2 * BLOCK), 1)
            bias_ref[1] = jnp.where(keys >= jnp.where(first, BLOCK, 0), bias_ref[0], NEG_INF)
        pos1 = seq_tile * tile + lax.broadcasted_iota(jnp.int32, (tile, 1), 0) + 1
        ext = uext_ref[:, cols]
        s, shift = ext, 1
        while shift < win:
            s = s + pltpu.roll(s, shift, axis=0)
            shift *= 2
        s, tok = s[POOL_HALO:], ext[POOL_HALO:]
        inv_cnt = 1.0 / jnp.minimum(pos1, win).astype(F32)
        d = (s * inv_cnt - tok).astype(BF16)
        yield
        y = jnp.dot(d, pw_ref[g], preferred_element_type=F32)
        pool_ref[:, cols] = y * ps_ref[:, cols]

    for g, win in enumerate(POOL_WINDOWS):
        pool_tasks.append(functools.partial(pool_group, g, win))

    def attn_unit(qb, j):
        low = lax.broadcasted_iota(jnp.int32, (1, LANES), 1) < HEAD_DIM
        top = lax.broadcasted_iota(jnp.int32, (2 * BLOCK, 1), 0) < BLOCK
        zero_bf = jnp.zeros((), BF16)
        r0 = qb * BLOCK
        if qb == 0:
            kv_band = jnp.concatenate([kvh_ref[...], kv_ref[0:BLOCK, :]], axis=0)
        else:
            kv_band = kv_ref[r0 - BLOCK:r0 + BLOCK, :]
        kvh = j // (ATTN_WIDTH // LANES // N_KV_HEADS)
        kd = kv_band[:, kvh * LANES:(kvh + 1) * LANES]
        vd = kv_band[:, (N_KV_HEADS + kvh) * LANES:(N_KV_HEADS + kvh + 1) * LANES]
        qblk = q_ref[r0:r0 + BLOCK, j * LANES:(j + 1) * LANES]
        qq = jnp.concatenate([jnp.where(low, qblk, zero_bf), jnp.where(low, zero_bf, qblk)], axis=0)
        s = lax.dot_general(qq, kd, (((1,), (1,)), ((), ())), preferred_element_type=F32)
        s = s + bias_ref[1 if qb == 0 else 0]
        sink = jnp.where(top, sinks_ref[0, 2 * j], sinks_ref[0, 2 * j + 1])
        m = jnp.maximum(jnp.max(s, axis=-1, keepdims=True), sink)
        e = jnp.exp(s - m).astype(BF16)
        sink_e = jnp.exp(sink - m)
        yield
        v_ones = jnp.concatenate([vd, jnp.ones((2 * BLOCK, LANES), BF16)], axis=1)
        o = jnp.dot(e, v_ones, preferred_element_type=F32)
        o = o[:, :LANES] * (1.0 / (o[:, LANES:] + sink_e))
        attn_ref[r0:r0 + BLOCK, j * LANES:(j + 1) * LANES] = jnp.where(low, o[:BLOCK], o[BLOCK:])

    for qb in range(tile // BLOCK):
        for j in range(ATTN_WIDTH // LANES):
            attn_tasks.append(functools.partial(attn_unit, qb, j))

    def pool_norm():
        merged_ref[:, :POOL_WIDTH] = _rms(pool_ref[...], pon_ref[...]).astype(BF16)
        yield

    def attn_norm():
        merged_ref[:, POOL_WIDTH:] = _rms(attn_ref[...], aon_ref[...]).astype(BF16)
        yield

    tasks = []
    per_pool = -(-len(attn_tasks) // len(pool_tasks))
    while pool_tasks or attn_tasks:
        if pool_tasks:
            tasks.append(pool_tasks.pop(0))
        tasks.extend(attn_tasks[:per_pool])
        del attn_tasks[:per_pool]
    return tasks + [pool_norm, attn_norm]


def _stage2_kernel(sinks_ref, x1_ref, u_ref, uh_ref, q_ref, kv_ref, kvh_ref, p_ref,
                   pw_ref, ps_ref, pon_ref, aon_ref, wout_ref,
                   n2_ref, wg_ref, wu_ref, wd_ref, pn_ref, wpg_ref, wpp_ref,
                   out_ref, uext_ref, merged_ref, pool_ref, attn_ref, bias_ref, *, tiles_per_seq, n_tiles):
    i = pl.program_id(0)

    @pl.when(i == 0)
    def _():
        merged_ref[...] = jnp.zeros_like(merged_ref)
        rows = lax.broadcasted_iota(jnp.int32, (2 * BLOCK, 2 * BLOCK), 0)
        keys = lax.broadcasted_iota(jnp.int32, (2 * BLOCK, 2 * BLOCK), 1)
        qpos = jnp.where(rows >= BLOCK, rows - BLOCK, rows)
        bias_ref[0] = jnp.where((keys > qpos) & (keys <= qpos + BLOCK), 0.0, NEG_INF)

    seq_tile = lax.rem(jnp.minimum(i, n_tiles - 1), tiles_per_seq)
    tasks = _mixer_tasks(seq_tile == 0, seq_tile, sinks_ref, u_ref, uh_ref, q_ref, kv_ref, kvh_ref,
                         pw_ref, ps_ref, pon_ref, aon_ref, uext_ref, pool_ref, attn_ref, bias_ref, merged_ref)
    n_slots = 1 + 3 * len(_ff_chunks())
    starts = [len(tasks) // n_slots + (s < len(tasks) % n_slots) for s in range(n_slots)]
    pending = []

    def slot(n_new=None):
        while pending:
            for _ in pending.pop(0):
                pass
        for _ in range(starts.pop(0) if n_new is None else n_new):
            gen = tasks.pop(0)()
            next(gen)
            pending.append(gen)

    x2 = x1_ref[...] + jnp.dot(merged_ref[...], wout_ref[...], preferred_element_type=F32)
    slot()
    x3 = _ffn_half_step(x2, n2_ref[...], wg_ref, wu_ref, wd_ref, slot)
    slot(0)
    assert not tasks and not starts
    hg = _rms(x3, pn_ref[...]).astype(BF16)
    gate = jax.nn.sigmoid(jnp.dot(hg, wpg_ref[...], preferred_element_type=F32))
    emb = jnp.dot(p_ref[...].astype(BF16), wpp_ref[...], preferred_element_type=F32)
    out_ref[...] = x3 + gate * emb


def _resident(shape):
    nd = len(shape)
    return pl.BlockSpec(shape, lambda i: (0,) * nd, pipeline_mode=pl.Buffered(1))


def _rows(tile, width):
    return pl.BlockSpec((tile, width), lambda i: (i, 0))


def _stage1(x, n1, wg, wu, wd, nm, win, qg2, kg2):
    n = x.shape[0]
    tile = TOKEN_TILE
    return pl.pallas_call(
        _stage1_kernel,
        grid=(n // tile,),
        in_specs=[_rows(tile, D_MODEL), _resident(n1.shape), _resident(wg.shape), _resident(wu.shape),
                  _resident(wd.shape), _resident(nm.shape), _resident(win.shape),
                  _resident(qg2.shape), _resident(kg2.shape)],
        out_specs=[_rows(tile, D_MODEL), _rows(tile, POOL_WIDTH), _rows(tile, ATTN_WIDTH),
                   _rows(tile, 4 * LANES)],
        out_shape=[jax.ShapeDtypeStruct((n, D_MODEL), F32), jax.ShapeDtypeStruct((n, POOL_WIDTH), F32),
                   jax.ShapeDtypeStruct((n, ATTN_WIDTH), BF16), jax.ShapeDtypeStruct((n, 4 * LANES), BF16)],
        compiler_params=pltpu.CompilerParams(dimension_semantics=("arbitrary",),
                                             vmem_limit_bytes=VMEM_LIMIT_BYTES),
        name="ffn1_mix_in",
    )(x, n1, wg, wu, wd, nm, win, qg2, kg2)


def _stage2(sinks, x1, u, q, kv, p, pw, ps, pon, aon, wout, n2, wg, wu, wd, pn, wpg, wpp, seq):
    n = x1.shape[0]
    tile = TOKEN_TILE
    tiles_per_seq = seq // tile
    n_tiles = n // tile
    halo_u = tile // POOL_HALO
    halo_kv = tile // BLOCK
    res = [pw, ps, pon, aon, wout, n2, wg, wu, wd, pn, wpg, wpp]
    mix = lambda i: jnp.minimum(i, n_tiles - 1)
    tail = lambda i: jnp.maximum(i - 1, 0)
    mix_rows = lambda width: pl.BlockSpec((tile, width), lambda i: (mix(i), 0))
    tail_rows = lambda width: pl.BlockSpec((tile, width), lambda i: (tail(i), 0))
    return pl.pallas_call(
        functools.partial(_stage2_kernel, tiles_per_seq=tiles_per_seq, n_tiles=n_tiles),
        grid=(n_tiles + 1,),
        in_specs=[pl.BlockSpec(memory_space=pltpu.SMEM),
                  tail_rows(D_MODEL),
                  mix_rows(POOL_WIDTH),
                  pl.BlockSpec((POOL_HALO, POOL_WIDTH), lambda i: (jnp.maximum(mix(i) * halo_u - 1, 0), 0)),
                  mix_rows(ATTN_WIDTH),
                  mix_rows(4 * LANES),
                  pl.BlockSpec((BLOCK, 4 * LANES), lambda i: (jnp.maximum(mix(i) * halo_kv - 1, 0), 0)),
                  tail_rows(PLE_DIM)] + [_resident(a.shape) for a in res],
        out_specs=tail_rows(D_MODEL),
        out_shape=jax.ShapeDtypeStruct((n, D_MODEL), F32),
        scratch_shapes=[pltpu.VMEM((tile + POOL_HALO, POOL_WIDTH), F32),
                        pltpu.VMEM((tile, POOL_WIDTH + ATTN_WIDTH), BF16),
                        pltpu.VMEM((tile, POOL_WIDTH), F32),
                        pltpu.VMEM((tile, ATTN_WIDTH), F32),
                        pltpu.VMEM((2, 2 * BLOCK, 2 * BLOCK), F32)],
        compiler_params=pltpu.CompilerParams(dimension_semantics=("arbitrary",),
                                             vmem_limit_bytes=VMEM_LIMIT_BYTES),
        name="mix_ffn2_ple",
    )(sinks, x1, u, u, q, kv, kv, p, *res)


def kernel(x, p, ffn1_norm, ffn1_w_gate, ffn1_w_up, ffn1_w_down, mix_norm, w_in, pool_w, pool_scale, q_norm, k_norm, attn_sinks, pool_out_norm, attn_out_norm, w_out, ffn2_norm, ffn2_w_gate, ffn2_w_up, ffn2_w_down, ple_norm, w_ple_gate, w_ple_proj):
    b, s, d = x.shape
    assert d == D_MODEL and s % TOKEN_TILE == 0 and TOKEN_TILE % BLOCK == 0
    depth = p.shape[0]
    xf = x.reshape(b * s, d)
    row = lambda a: a.reshape(1, -1)
    for i in range(depth):
        x1, u, q, kv = _stage1(
            xf, row(ffn1_norm[i]), ffn1_w_gate[i].astype(BF16), ffn1_w_up[i].astype(BF16),
            ffn1_w_down[i].astype(BF16), row(mix_norm[i]), w_in[i].astype(BF16),
            row(jnp.tile(q_norm[i], 2)), row(jnp.tile(k_norm[i], 2)))
        xf = _stage2(
            row(attn_sinks[i]), x1, u, q, kv, p[i].reshape(b * s, PLE_DIM),
            pool_w[i].astype(BF16), row(pool_scale[i]), row(pool_out_norm[i]), row(attn_out_norm[i]),
            w_out[i].astype(BF16), row(ffn2_norm[i]), ffn2_w_gate[i].astype(BF16),
            ffn2_w_up[i].astype(BF16), ffn2_w_down[i].astype(BF16), row(ple_norm[i]),
            w_ple_gate[i].astype(BF16), w_ple_proj[i].astype(BF16), s)
    return xf.reshape(b, s, d)
```

```python
import functools
import math

import jax
import jax.numpy as jnp
from jax import lax
from jax.experimental import pallas as pl
from jax.experimental.pallas import tpu as pltpu

D_MODEL = 1024
PLE_DIM = 256
D_FF = 2816
POOL_WIDTH = 512
POOL_WINDOWS = (2, 4, 8, 16)
POOL_GROUP_DIM = 128
HEAD_DIM = 64
N_Q_HEADS = 8
N_KV_HEADS = 2
ATTN_WIDTH = N_Q_HEADS * HEAD_DIM
KV_WIDTH = N_KV_HEADS * HEAD_DIM
WINDOW = 128
BLOCK = 128
MIX_IN = POOL_WIDTH + ATTN_WIDTH + 2 * KV_WIDTH
EPS = 1e-6
NEG_INF = -1e30

LANES = 128
MXU_DIM = 256
POOL_HALO = 16
TOKEN_TILE = 512
FF_CHUNK = 2 * MXU_DIM
VMEM_LIMIT_BYTES = 56 * 1024 * 1024

F32 = jnp.float32
BF16 = jnp.bfloat16


def _ff_chunks():
    chunks, lo = [], 0
    while lo < D_FF:
        sz = min(FF_CHUNK, D_FF - lo)
        chunks.append((lo, sz))
        lo += sz
    return chunks


N_FFN_SLOTS = 3 * len(_ff_chunks())


def _rms(x, g):
    return x * lax.rsqrt(jnp.mean(x * x, axis=-1, keepdims=True) + EPS) * g


class _Slots:
    def __init__(self, tasks, n_slots, spread=True):
        self.tasks = list(tasks)
        if spread:
            self.starts = [len(tasks) // n_slots + (s < len(tasks) % n_slots) for s in range(n_slots)]
        else:
            self.starts = [1] * len(tasks) + [0] * (n_slots - len(tasks))
        self.pending = []

    def __call__(self):
        self.flush()
        for _ in range(self.starts.pop(0) if self.starts else 0):
            gen = self.tasks.pop(0)()
            next(gen)
            self.pending.append(gen)

    def flush(self):
        while self.pending:
            for _ in self.pending.pop(0):
                pass

    def finish(self):
        self.flush()
        assert not self.tasks and not self.starts


def _swiglu_chunks(h, wg_ref, wu_ref, wd_ref, slot, before_last_down=None):
    acc, act_prev, rows_prev = None, None, None

    def down(act, lo, sz, acc):
        part = jnp.dot(act, wd_ref[lo:lo + sz, :], preferred_element_type=F32)
        slot()
        return part if acc is None else acc + part

    for lo, sz in _ff_chunks():
        gate = jnp.dot(h, wg_ref[:, lo:lo + sz], preferred_element_type=F32)
        slot()
        up = jnp.dot(h, wu_ref[:, lo:lo + sz], preferred_element_type=F32)
        slot()
        if act_prev is not None:
            acc = down(act_prev, *rows_prev, acc)
        act_prev, rows_prev = (gate * jax.nn.sigmoid(gate) * up).astype(BF16), (lo, sz)
    if before_last_down is not None:
        before_last_down()
    return down(act_prev, *rows_prev, acc)


def _head_pair_norm(zb, gain2, low):
    sq = zb * zb
    ms_lo = jnp.sum(jnp.where(low, sq, 0.0), axis=-1, keepdims=True) * (1.0 / HEAD_DIM)
    ms_hi = jnp.sum(jnp.where(low, 0.0, sq), axis=-1, keepdims=True) * (1.0 / HEAD_DIM)
    r = jnp.where(low, lax.rsqrt(ms_lo + EPS), lax.rsqrt(ms_hi + EPS))
    return zb * r * gain2


def _dup_halves(a, low):
    rot = pltpu.roll(a, HEAD_DIM, axis=1)
    return jnp.where(low, a, rot), jnp.where(low, rot, a)


def _mix_in_tasks(x1_scr, nm_ref, win_ref, qg_ref, kg_ref, u_ref, q_ref, kv_ref):
    state = {}
    o_k = POOL_WIDTH + ATTN_WIDTH
    scale = 1.0 / math.sqrt(HEAD_DIM)

    def low():
        return lax.broadcasted_iota(jnp.int32, (1, LANES), 1) < HEAD_DIM

    def project():
        h = _rms(x1_scr[...], nm_ref[...]).astype(BF16)
        yield
        state["z"] = jnp.dot(h, win_ref[...], preferred_element_type=F32)

    def pool_in():
        u_ref[...] = state["z"][:, :POOL_WIDTH]
        yield

    def q_block(j):
        lo = POOL_WIDTH + j * LANES
        qn = _head_pair_norm(state["z"][:, lo:lo + LANES], qg_ref[...], low()) * scale
        q_ref[:, j * LANES:(j + 1) * LANES] = qn.astype(BF16)
        yield

    def k_block():
        kn = _head_pair_norm(state["z"][:, o_k:o_k + KV_WIDTH], kg_ref[...], low())
        for j, a in enumerate(_dup_halves(kn, low())):
            kv_ref[:, j * LANES:(j + 1) * LANES] = a.astype(BF16)
        yield

    def v_block():
        for j, a in enumerate(_dup_halves(state["z"][:, o_k + KV_WIDTH:o_k + 2 * KV_WIDTH], low())):
            kv_ref[:, (N_KV_HEADS + j) * LANES:(N_KV_HEADS + j + 1) * LANES] = a.astype(BF16)
        yield

    return ([project, pool_in] + [functools.partial(q_block, j) for j in range(ATTN_WIDTH // LANES)]
            + [k_block, v_block])


def _stage1_kernel(x_ref, n1_ref, wg_ref, wu_ref, wd_ref, nm_ref, win_ref, qg_ref, kg_ref,
                   x1_ref, u_ref, q_ref, kv_ref, x1_scr):
    i = pl.program_id(0)

    @pl.when(i == 0)
    def _():
        x1_scr[...] = jnp.zeros_like(x1_scr)

    tasks = _mix_in_tasks(x1_scr, nm_ref, win_ref, qg_ref, kg_ref, u_ref, q_ref, kv_ref)
    slots = _Slots(tasks, N_FFN_SLOTS, spread=False)
    x = x_ref[...]
    h = _rms(x, n1_ref[...]).astype(BF16)
    acc = _swiglu_chunks(h, wg_ref, wu_ref, wd_ref, slots)
    slots.finish()
    x1 = x + 0.5 * acc
    x1_ref[...] = x1
    x1_scr[...] = x1


def _mixer_tasks(first, seq_tile, sinks_ref, u_ref, uh_ref, q_ref, kv_ref, kvh_ref,
                 pw_ref, ps_ref, pon_ref, aon_ref, uext_ref, pool_ref, attn_ref, bias_ref, merged_ref):
    tile = u_ref.shape[0]
    pool_tasks, attn_tasks = [], []

    def pool_group(g, win):
        cols = slice(g * POOL_GROUP_DIM, (g + 1) * POOL_GROUP_DIM)
        if g == 0:
            uext_ref[0:POOL_HALO, :] = jnp.where(first, 0.0, uh_ref[...])
            uext_ref[POOL_HALO:, :] = u_ref[...]
            keys = lax.broadcasted_iota(jnp.int32, (2 * BLOCK, 2 * BLOCK), 1)
            bias_ref[1] = jnp.where(keys >= jnp.where(first, BLOCK, 0), bias_ref[0], NEG_INF)
        pos1 = seq_tile * tile + lax.broadcasted_iota(jnp.int32, (tile, 1), 0) + 1
        ext = uext_ref[:, cols]
        s, shift = ext, 1
        while shift < win:
            s = s + pltpu.roll(s, shift, axis=0)
            shift *= 2
        s, tok = s[POOL_HALO:], ext[POOL_HALO:]
        inv_cnt = 1.0 / jnp.minimum(pos1, win).astype(F32)
        d = (s * inv_cnt - tok).astype(BF16)
        yield
        y = jnp.dot(d, pw_ref[g], preferred_element_type=F32)
        pool_ref[:, cols] = y * ps_ref[:, cols]

    for g, win in enumerate(POOL_WINDOWS):
        pool_tasks.append(functools.partial(pool_group, g, win))

    def attn_unit(qb, j):
        low = lax.broadcasted_iota(jnp.int32, (1, LANES), 1) < HEAD_DIM
        top = lax.broadcasted_iota(jnp.int32, (2 * BLOCK, 1), 0) < BLOCK
        zero_bf = jnp.zeros((), BF16)
        r0 = qb * BLOCK
        if qb == 0:
            kv_band = jnp.concatenate([kvh_ref[...], kv_ref[0:BLOCK, :]], axis=0)
        else:
            kv_band = kv_ref[r0 - BLOCK:r0 + BLOCK, :]
        kvh = j // (ATTN_WIDTH // LANES // N_KV_HEADS)
        kd = kv_band[:, kvh * LANES:(kvh + 1) * LANES]
        vd = kv_band[:, (N_KV_HEADS + kvh) * LANES:(N_KV_HEADS + kvh + 1) * LANES]
        qblk = q_ref[r0:r0 + BLOCK, j * LANES:(j + 1) * LANES]
        qq = jnp.concatenate([jnp.where(low, qblk, zero_bf), jnp.where(low, zero_bf, qblk)], axis=0)
        s = lax.dot_general(qq, kd, (((1,), (1,)), ((), ())), preferred_element_type=F32)
        s = s + bias_ref[1 if qb == 0 else 0]
        sink = jnp.where(top, sinks_ref[0, 2 * j], sinks_ref[0, 2 * j + 1])
        m = jnp.maximum(jnp.max(s, axis=-1, keepdims=True), sink)
        e = jnp.exp(s - m).astype(BF16)
        sink_e = jnp.exp(sink - m)
        yield
        v_ones = jnp.concatenate([vd, jnp.ones((2 * BLOCK, LANES), BF16)], axis=1)
        o = jnp.dot(e, v_ones, preferred_element_type=F32)
        o = o[:, :LANES] * (1.0 / (o[:, LANES:] + sink_e))
        attn_ref[r0:r0 + BLOCK, j * LANES:(j + 1) * LANES] = jnp.where(low, o[:BLOCK], o[BLOCK:])

    for qb in range(tile // BLOCK):
        for j in range(ATTN_WIDTH // LANES):
            attn_tasks.append(functools.partial(attn_unit, qb, j))

    def pool_norm():
        merged_ref[:, :POOL_WIDTH] = _rms(pool_ref[...], pon_ref[...]).astype(BF16)
        yield

    def attn_norm():
        merged_ref[:, POOL_WIDTH:] = _rms(attn_ref[...], aon_ref[...]).astype(BF16)
        yield

    tasks = []
    per_pool = -(-len(attn_tasks) // len(pool_tasks))
    while pool_tasks or attn_tasks:
        if pool_tasks:
            tasks.append(pool_tasks.pop(0))
        tasks.extend(attn_tasks[:per_pool])
        del attn_tasks[:per_pool]
    return tasks + [pool_norm, attn_norm]


def _stage2_kernel(sinks_ref, x1_ref, u_ref, uh_ref, q_ref, kv_ref, kvh_ref, p_ref,
                   pw_ref, ps_ref, pon_ref, aon_ref, wout_ref,
                   n2_ref, wg_ref, wu_ref, wd_ref, pn_ref, wpg_ref, wpp_ref,
                   out_ref, uext_ref, merged_ref, pool_ref, attn_ref, bias_ref, x2_scr, h_scr,
                   *, tiles_per_seq, n_tiles):
    i = pl.program_id(0)

    @pl.when(i == 0)
    def _():
        x2_scr[...] = jnp.zeros_like(x2_scr)
        h_scr[...] = jnp.zeros_like(h_scr)
        rows = lax.broadcasted_iota(jnp.int32, (2 * BLOCK, 2 * BLOCK), 0)
        keys = lax.broadcasted_iota(jnp.int32, (2 * BLOCK, 2 * BLOCK), 1)
        qpos = jnp.where(rows >= BLOCK, rows - BLOCK, rows)
        bias_ref[0] = jnp.where((keys > qpos) & (keys <= qpos + BLOCK), 0.0, NEG_INF)

    seq_tile = lax.rem(jnp.minimum(i, n_tiles - 1), tiles_per_seq)
    tasks = _mixer_tasks(seq_tile == 0, seq_tile, sinks_ref, u_ref, uh_ref, q_ref, kv_ref, kvh_ref,
                         pw_ref, ps_ref, pon_ref, aon_ref, uext_ref, pool_ref, attn_ref, bias_ref, merged_ref)
    slots = _Slots(tasks, N_FFN_SLOTS - 1)
    half = D_MODEL // 2
    proj = []

    def first_proj_half():
        slots.finish()
        proj.append(jnp.dot(merged_ref[...], wout_ref[:, :half], preferred_element_type=F32))

    acc = _swiglu_chunks(h_scr[...], wg_ref, wu_ref, wd_ref, slots, first_proj_half)
    x3 = x2_scr[...] + 0.5 * acc
    proj.append(jnp.dot(merged_ref[...], wout_ref[:, half:], preferred_element_type=F32))
    hg = _rms(x3, pn_ref[...]).astype(BF16)
    gate_a = jnp.dot(hg, wpg_ref[:, :half], preferred_element_type=F32)
    x2 = x1_ref[...] + jnp.concatenate(proj, axis=1)
    x2_scr[...] = x2
    h_scr[...] = _rms(x2, n2_ref[...]).astype(BF16)
    emb = jnp.dot(p_ref[...].astype(BF16), wpp_ref[...], preferred_element_type=F32)
    gate_b = jnp.dot(hg, wpg_ref[:, half:], preferred_element_type=F32)
    out_ref[:, :half] = x3[:, :half] + jax.nn.sigmoid(gate_a) * emb[:, :half]
    out_ref[:, half:] = x3[:, half:] + jax.nn.sigmoid(gate_b) * emb[:, half:]


def _resident(shape):
    nd = len(shape)
    return pl.BlockSpec(shape, lambda i: (0,) * nd, pipeline_mode=pl.Buffered(1))


def _stage1(x, n1, wg, wu, wd, nm, win, qg2, kg2):
    n = x.shape[0]
    tile = TOKEN_TILE
    n_tiles = n // tile
    cur_rows = lambda width: pl.BlockSpec((tile, width), lambda i: (jnp.minimum(i, n_tiles - 1), 0))
    prev_rows = lambda width: pl.BlockSpec((tile, width), lambda i: (jnp.maximum(i - 1, 0), 0))
    return pl.pallas_call(
        _stage1_kernel,
        grid=(n_tiles + 1,),
        in_specs=[cur_rows(D_MODEL), _resident(n1.shape), _resident(wg.shape), _resident(wu.shape),
                  _resident(wd.shape), _resident(nm.shape), _resident(win.shape),
                  _resident(qg2.shape), _resident(kg2.shape)],
        out_specs=[cur_rows(D_MODEL), prev_rows(POOL_WIDTH), prev_rows(ATTN_WIDTH), prev_rows(4 * LANES)],
        out_shape=[jax.ShapeDtypeStruct((n, D_MODEL), F32), jax.ShapeDtypeStruct((n, POOL_WIDTH), F32),
                   jax.ShapeDtypeStruct((n, ATTN_WIDTH), BF16), jax.ShapeDtypeStruct((n, 4 * LANES), BF16)],
        scratch_shapes=[pltpu.VMEM((tile, D_MODEL), F32)],
        compiler_params=pltpu.CompilerParams(dimension_semantics=("arbitrary",),
                                             vmem_limit_bytes=VMEM_LIMIT_BYTES),
        name="ffn1_mix_in",
    )(x, n1, wg, wu, wd, nm, win, qg2, kg2)


def _stage2(sinks, x1, u, q, kv, p, pw, ps, pon, aon, wout, n2, wg, wu, wd, pn, wpg, wpp, seq):
    n = x1.shape[0]
    tile = TOKEN_TILE
    tiles_per_seq = seq // tile
    n_tiles = n // tile
    halo_u = tile // POOL_HALO
    halo_kv = tile // BLOCK
    res = [pw, ps, pon, aon, wout, n2, wg, wu, wd, pn, wpg, wpp]
    mix = lambda i: jnp.minimum(i, n_tiles - 1)
    tail = lambda i: jnp.maximum(i - 1, 0)
    mix_rows = lambda width: pl.BlockSpec((tile, width), lambda i: (mix(i), 0))
    tail_rows = lambda width: pl.BlockSpec((tile, width), lambda i: (tail(i), 0))
    return pl.pallas_call(
        functools.partial(_stage2_kernel, tiles_per_seq=tiles_per_seq, n_tiles=n_tiles),
        grid=(n_tiles + 1,),
        in_specs=[pl.BlockSpec(memory_space=pltpu.SMEM),
                  mix_rows(D_MODEL),
                  mix_rows(POOL_WIDTH),
                  pl.BlockSpec((POOL_HALO, POOL_WIDTH), lambda i: (jnp.maximum(mix(i) * halo_u - 1, 0), 0)),
                  mix_rows(ATTN_WIDTH),
                  mix_rows(4 * LANES),
                  pl.BlockSpec((BLOCK, 4 * LANES), lambda i: (jnp.maximum(mix(i) * halo_kv - 1, 0), 0)),
                  tail_rows(PLE_DIM)] + [_resident(a.shape) for a in res],
        out_specs=tail_rows(D_MODEL),
        out_shape=jax.ShapeDtypeStruct((n, D_MODEL), F32),
        scratch_shapes=[pltpu.VMEM((tile + POOL_HALO, POOL_WIDTH), F32),
                        pltpu.VMEM((tile, POOL_WIDTH + ATTN_WIDTH), BF16),
                        pltpu.VMEM((tile, POOL_WIDTH), F32),
                        pltpu.VMEM((tile, ATTN_WIDTH), F32),
                        pltpu.VMEM((2, 2 * BLOCK, 2 * BLOCK), F32),
                        pltpu.VMEM((tile, D_MODEL), F32),
                        pltpu.VMEM((tile, D_MODEL), BF16)],
        compiler_params=pltpu.CompilerParams(dimension_semantics=("arbitrary",),
                                             vmem_limit_bytes=VMEM_LIMIT_BYTES),
        name="mix_ffn2_ple",
    )(sinks, x1, u, u, q, kv, kv, p, *res)


def kernel(x, p, ffn1_norm, ffn1_w_gate, ffn1_w_up, ffn1_w_down, mix_norm, w_in, pool_w, pool_scale, q_norm, k_norm, attn_sinks, pool_out_norm, attn_out_norm, w_out, ffn2_norm, ffn2_w_gate, ffn2_w_up, ffn2_w_down, ple_norm, w_ple_gate, w_ple_proj):
    b, s, d = x.shape
    assert d == D_MODEL and s % TOKEN_TILE == 0 and TOKEN_TILE % BLOCK == 0
    depth = p.shape[0]
    xf = x.reshape(b * s, d)
    row = lambda a: a.reshape(1, -1)
    for i in range(depth):
        x1, u, q, kv = _stage1(
            xf, row(ffn1_norm[i]), ffn1_w_gate[i].astype(BF16), ffn1_w_up[i].astype(BF16),
            ffn1_w_down[i].astype(BF16), row(mix_norm[i]), w_in[i].astype(BF16),
            row(jnp.tile(q_norm[i], 2)), row(jnp.tile(k_norm[i], 2)))
        xf = _stage2(
            row(attn_sinks[i]), x1, u, q, kv, p[i].reshape(b * s, PLE_DIM),
            pool_w[i].astype(BF16), row(pool_scale[i]), row(pool_out_norm[i]), row(attn_out_norm[i]),
            w_out[i].astype(BF16), row(ffn2_norm[i]), ffn2_w_gate[i].astype(BF16),
            ffn2_w_up[i].astype(BF16), ffn2_w_down[i].astype(BF16), row(ple_norm[i]),
            w_ple_gate[i].astype(BF16), w_ple_proj[i].astype(BF16), s)
    return xf.reshape(b, s, d)
```

```python
import functools
import math

import jax
import jax.numpy as jnp
from jax import lax
from jax.experimental import pallas as pl
from jax.experimental.pallas import tpu as pltpu

D_MODEL = 1024
PLE_DIM = 256
D_FF = 2816
POOL_WIDTH = 512
POOL_WINDOWS = (2, 4, 8, 16)
POOL_GROUP_DIM = 128
HEAD_DIM = 64
N_Q_HEADS = 8
N_KV_HEADS = 2
ATTN_WIDTH = N_Q_HEADS * HEAD_DIM
KV_WIDTH = N_KV_HEADS * HEAD_DIM
WINDOW = 128
BLOCK = 128
MIX_IN = POOL_WIDTH + ATTN_WIDTH + 2 * KV_WIDTH
EPS = 1e-6
NEG_INF = -1e30

LANES = 128
MXU_DIM = 256
POOL_HALO = 16
TOKEN_TILE = 512
FF_CHUNK = 2 * MXU_DIM
VMEM_LIMIT_BYTES = 56 * 1024 * 1024

F32 = jnp.float32
BF16 = jnp.bfloat16


def _ff_chunks():
    chunks, lo = [], 0
    while lo < D_FF:
        sz = min(FF_CHUNK, D_FF - lo)
        chunks.append((lo, sz))
        lo += sz
    return chunks


N_FFN_SLOTS = 3 * len(_ff_chunks())


def _rms(x, g):
    return x * lax.rsqrt(jnp.mean(x * x, axis=-1, keepdims=True) + EPS) * g


class _Slots:
    def __init__(self, tasks, n_slots, spread=True):
        self.tasks = list(tasks)
        if spread:
            self.starts = [len(tasks) // n_slots + (s < len(tasks) % n_slots) for s in range(n_slots)]
        else:
            self.starts = [1] * len(tasks) + [0] * (n_slots - len(tasks))
        self.pending = []

    def __call__(self):
        self.flush()
        for _ in range(self.starts.pop(0) if self.starts else 0):
            gen = self.tasks.pop(0)()
            next(gen)
            self.pending.append(gen)

    def flush(self):
        while self.pending:
            for _ in self.pending.pop(0):
                pass

    def finish(self):
        self.flush()
        assert not self.tasks and not self.starts


def _swiglu_chunks(h, wg_ref, wu_ref, wd_ref, slot, before_last_down=None):
    acc, act_prev, rows_prev = None, None, None

    def down(act, lo, sz, acc):
        part = jnp.dot(act, wd_ref[lo:lo + sz, :], preferred_element_type=F32)
        slot()
        return part if acc is None else acc + part

    for lo, sz in _ff_chunks():
        gate = jnp.dot(h, wg_ref[:, lo:lo + sz], preferred_element_type=F32)
        slot()
        up = jnp.dot(h, wu_ref[:, lo:lo + sz], preferred_element_type=F32)
        slot()
        if act_prev is not None:
            acc = down(act_prev, *rows_prev, acc)
        act_prev, rows_prev = (gate * jax.nn.sigmoid(gate) * up).astype(BF16), (lo, sz)
    if before_last_down is not None:
        before_last_down()
    return down(act_prev, *rows_prev, acc)


def _head_pair_norm(zb, gain2, low):
    sq = zb * zb
    ms_lo = jnp.sum(jnp.where(low, sq, 0.0), axis=-1, keepdims=True) * (1.0 / HEAD_DIM)
    ms_hi = jnp.sum(jnp.where(low, 0.0, sq), axis=-1, keepdims=True) * (1.0 / HEAD_DIM)
    r = jnp.where(low, lax.rsqrt(ms_lo + EPS), lax.rsqrt(ms_hi + EPS))
    return zb * r * gain2


def _dup_halves(a, low):
    rot = pltpu.roll(a, HEAD_DIM, axis=1)
    return jnp.where(low, a, rot), jnp.where(low, rot, a)


def _mix_in_tasks(x1_scr, nm_ref, win_ref, qg_ref, kg_ref, u_ref, q_ref, kv_ref):
    state = {}
    o_k = POOL_WIDTH + ATTN_WIDTH
    scale = 1.0 / math.sqrt(HEAD_DIM)

    def low():
        return lax.broadcasted_iota(jnp.int32, (1, LANES), 1) < HEAD_DIM

    def project():
        h = _rms(x1_scr[...], nm_ref[...]).astype(BF16)
        yield
        state["z"] = jnp.dot(h, win_ref[...], preferred_element_type=F32)

    def pool_in():
        u_ref[...] = state["z"][:, :POOL_WIDTH]
        yield

    def q_block(j):
        lo = POOL_WIDTH + j * LANES
        qn = _head_pair_norm(state["z"][:, lo:lo + LANES], qg_ref[...], low()) * scale
        q_ref[:, j * LANES:(j + 1) * LANES] = qn.astype(BF16)
        yield

    def k_block():
        kn = _head_pair_norm(state["z"][:, o_k:o_k + KV_WIDTH], kg_ref[...], low())
        for j, a in enumerate(_dup_halves(kn, low())):
            kv_ref[:, j * LANES:(j + 1) * LANES] = a.astype(BF16)
        yield

    def v_block():
        for j, a in enumerate(_dup_halves(state["z"][:, o_k + KV_WIDTH:o_k + 2 * KV_WIDTH], low())):
            kv_ref[:, (N_KV_HEADS + j) * LANES:(N_KV_HEADS + j + 1) * LANES] = a.astype(BF16)
        yield

    return ([project, pool_in] + [functools.partial(q_block, j) for j in range(ATTN_WIDTH // LANES)]
            + [k_block, v_block])


def _stage1_kernel(x_ref, n1_ref, wg_ref, wu_ref, wd_ref, nm_ref, win_ref, qg_ref, kg_ref,
                   x1_ref, u_ref, q_ref, kv_ref, x1_scr, *, n_tiles):
    i = pl.program_id(0)

    @pl.when(i == 0)
    def _():
        x1_scr[...] = jnp.zeros_like(x1_scr)

    @pl.when(i < n_tiles)
    def _():
        tasks = _mix_in_tasks(x1_scr, nm_ref, win_ref, qg_ref, kg_ref, u_ref, q_ref, kv_ref)
        slots = _Slots(tasks, N_FFN_SLOTS, spread=False)
        x = x_ref[...]
        h = _rms(x, n1_ref[...]).astype(BF16)
        acc = _swiglu_chunks(h, wg_ref, wu_ref, wd_ref, slots)
        slots.finish()
        x1 = x + 0.5 * acc
        x1_ref[...] = x1
        x1_scr[...] = x1

    @pl.when(i == n_tiles)
    def _():
        for task in _mix_in_tasks(x1_scr, nm_ref, win_ref, qg_ref, kg_ref, u_ref, q_ref, kv_ref):
            for _ in task():
                pass


def _mixer_tasks(first, seq_tile, sinks_ref, u_ref, uh_ref, q_ref, kv_ref, kvh_ref,
                 pw_ref, ps_ref, pon_ref, aon_ref, uext_ref, pool_ref, attn_ref, bias_ref, merged_ref):
    tile = u_ref.shape[0]
    pool_tasks, attn_tasks = [], []

    def pool_group(g, win):
        cols = slice(g * POOL_GROUP_DIM, (g + 1) * POOL_GROUP_DIM)
        if g == 0:
            uext_ref[0:POOL_HALO, :] = jnp.where(first, 0.0, uh_ref[...])
            uext_ref[POOL_HALO:, :] = u_ref[...]
            keys = lax.broadcasted_iota(jnp.int32, (2 * BLOCK, 2 * BLOCK), 1)
            bias_ref[1] = jnp.where(keys >= jnp.where(first, BLOCK, 0), bias_ref[0], NEG_INF)
        pos1 = seq_tile * tile + lax.broadcasted_iota(jnp.int32, (tile, 1), 0) + 1
        ext = uext_ref[:, cols]
        s, shift = ext, 1
        while shift < win:
            s = s + pltpu.roll(s, shift, axis=0)
            shift *= 2
        s, tok = s[POOL_HALO:], ext[POOL_HALO:]
        inv_cnt = 1.0 / jnp.minimum(pos1, win).astype(F32)
        d = (s * inv_cnt - tok).astype(BF16)
        yield
        y = jnp.dot(d, pw_ref[g], preferred_element_type=F32)
        pool_ref[:, cols] = y * ps_ref[:, cols]

    for g, win in enumerate(POOL_WINDOWS):
        pool_tasks.append(functools.partial(pool_group, g, win))

    def attn_unit(qb, j):
        low = lax.broadcasted_iota(jnp.int32, (1, LANES), 1) < HEAD_DIM
        top = lax.broadcasted_iota(jnp.int32, (2 * BLOCK, 1), 0) < BLOCK
        zero_bf = jnp.zeros((), BF16)
        r0 = qb * BLOCK
        if qb == 0:
            kv_band = jnp.concatenate([kvh_ref[...], kv_ref[0:BLOCK, :]], axis=0)
        else:
            kv_band = kv_ref[r0 - BLOCK:r0 + BLOCK, :]
        kvh = j // (ATTN_WIDTH // LANES // N_KV_HEADS)
        kd = kv_band[:, kvh * LANES:(kvh + 1) * LANES]
        vd = kv_band[:, (N_KV_HEADS + kvh) * LANES:(N_KV_HEADS + kvh + 1) * LANES]
        qblk = q_ref[r0:r0 + BLOCK, j * LANES:(j + 1) * LANES]
        qq = jnp.concatenate([jnp.where(low, qblk, zero_bf), jnp.where(low, zero_bf, qblk)], axis=0)
        s = lax.dot_general(qq, kd, (((1,), (1,)), ((), ())), preferred_element_type=F32)
        s = s + bias_ref[1 if qb == 0 else 0]
        sink = jnp.where(top, sinks_ref[0, 2 * j], sinks_ref[0, 2 * j + 1])
        m = jnp.maximum(jnp.max(s, axis=-1, keepdims=True), sink)
        e = jnp.exp(s - m).astype(BF16)
        sink_e = jnp.exp(sink - m)
        yield
        v_ones = jnp.concatenate([vd, jnp.ones((2 * BLOCK, LANES), BF16)], axis=1)
        o = jnp.dot(e, v_ones, preferred_element_type=F32)
        o = o[:, :LANES] * (1.0 / (o[:, LANES:] + sink_e))
        attn_ref[r0:r0 + BLOCK, j * LANES:(j + 1) * LANES] = jnp.where(low, o[:BLOCK], o[BLOCK:])

    for qb in range(tile // BLOCK):
        for j in range(ATTN_WIDTH // LANES):
            attn_tasks.append(functools.partial(attn_unit, qb, j))

    def pool_norm():
        merged_ref[:, :POOL_WIDTH] = _rms(pool_ref[...], pon_ref[...]).astype(BF16)
        yield

    def attn_norm():
        merged_ref[:, POOL_WIDTH:] = _rms(attn_ref[...], aon_ref[...]).astype(BF16)
        yield

    tasks = []
    per_pool = -(-len(attn_tasks) // len(pool_tasks))
    while pool_tasks or attn_tasks:
        if pool_tasks:
            tasks.append(pool_tasks.pop(0))
        tasks.extend(attn_tasks[:per_pool])
        del attn_tasks[:per_pool]
    return tasks + [pool_norm, attn_norm]


def _stage2_kernel(sinks_ref, x1_ref, u_ref, uh_ref, q_ref, kv_ref, kvh_ref, p_ref,
                   pw_ref, ps_ref, pon_ref, aon_ref, wout_ref,
                   n2_ref, wg_ref, wu_ref, wd_ref, pn_ref, wpg_ref, wpp_ref,
                   out_ref, uext_ref, merged_ref, pool_ref, attn_ref, bias_ref, x2_scr, h_scr,
                   *, tiles_per_seq, n_tiles):
    i = pl.program_id(0)

    seq_tile = lax.rem(jnp.minimum(i, n_tiles - 1), tiles_per_seq)
    half = D_MODEL // 2

    def mixer_tasks():
        return _mixer_tasks(seq_tile == 0, seq_tile, sinks_ref, u_ref, uh_ref, q_ref, kv_ref, kvh_ref, pw_ref,
                            ps_ref, pon_ref, aon_ref, uext_ref, pool_ref, attn_ref, bias_ref, merged_ref)

    def project(proj):
        x2 = x1_ref[...] + proj
        x2_scr[...] = x2
        h_scr[...] = _rms(x2, n2_ref[...]).astype(BF16)

    @pl.when(i == 0)
    def _():
        rows = lax.broadcasted_iota(jnp.int32, (2 * BLOCK, 2 * BLOCK), 0)
        keys = lax.broadcasted_iota(jnp.int32, (2 * BLOCK, 2 * BLOCK), 1)
        qpos = jnp.where(rows >= BLOCK, rows - BLOCK, rows)
        bias_ref[0] = jnp.where((keys > qpos) & (keys <= qpos + BLOCK), 0.0, NEG_INF)
        for task in mixer_tasks():
            for _ in task():
                pass
        project(jnp.dot(merged_ref[...], wout_ref[...], preferred_element_type=F32))

    @pl.when(i > 0)
    def _():
        slots = _Slots(mixer_tasks(), N_FFN_SLOTS - 1)
        proj = []

        def first_proj_half():
            slots.finish()
            proj.append(jnp.dot(merged_ref[...], wout_ref[:, :half], preferred_element_type=F32))

        acc = _swiglu_chunks(h_scr[...], wg_ref, wu_ref, wd_ref, slots, first_proj_half)
        x3 = x2_scr[...] + 0.5 * acc
        proj.append(jnp.dot(merged_ref[...], wout_ref[:, half:], preferred_element_type=F32))
        hg = _rms(x3, pn_ref[...]).astype(BF16)
        gate_a = jnp.dot(hg, wpg_ref[:, :half], preferred_element_type=F32)
        project(jnp.concatenate(proj, axis=1))
        emb = jnp.dot(p_ref[...].astype(BF16), wpp_ref[...], preferred_element_type=F32)
        gate_b = jnp.dot(hg, wpg_ref[:, half:], preferred_element_type=F32)
        out_ref[:, :half] = x3[:, :half] + jax.nn.sigmoid(gate_a) * emb[:, :half]
        out_ref[:, half:] = x3[:, half:] + jax.nn.sigmoid(gate_b) * emb[:, half:]


def _resident(shape):
    nd = len(shape)
    return pl.BlockSpec(shape, lambda i: (0,) * nd, pipeline_mode=pl.Buffered(1))


def _stage1(x, n1, wg, wu, wd, nm, win, qg2, kg2):
    n = x.shape[0]
    tile = TOKEN_TILE
    n_tiles = n // tile
    cur_rows = lambda width: pl.BlockSpec((tile, width), lambda i: (jnp.minimum(i, n_tiles - 1), 0))
    prev_rows = lambda width: pl.BlockSpec((tile, width), lambda i: (jnp.maximum(i - 1, 0), 0))
    return pl.pallas_call(
        functools.partial(_stage1_kernel, n_tiles=n_tiles),
        grid=(n_tiles + 1,),
        in_specs=[cur_rows(D_MODEL), _resident(n1.shape), _resident(wg.shape), _resident(wu.shape),
                  _resident(wd.shape), _resident(nm.shape), _resident(win.shape),
                  _resident(qg2.shape), _resident(kg2.shape)],
        out_specs=[cur_rows(D_MODEL), prev_rows(POOL_WIDTH), prev_rows(ATTN_WIDTH), prev_rows(4 * LANES)],
        out_shape=[jax.ShapeDtypeStruct((n, D_MODEL), F32), jax.ShapeDtypeStruct((n, POOL_WIDTH), F32),
                   jax.ShapeDtypeStruct((n, ATTN_WIDTH), BF16), jax.ShapeDtypeStruct((n, 4 * LANES), BF16)],
        scratch_shapes=[pltpu.VMEM((tile, D_MODEL), F32)],
        compiler_params=pltpu.CompilerParams(dimension_semantics=("arbitrary",),
                                             vmem_limit_bytes=VMEM_LIMIT_BYTES),
        name="ffn1_mix_in",
    )(x, n1, wg, wu, wd, nm, win, qg2, kg2)


def _stage2(sinks, x1, u, q, kv, p, pw, ps, pon, aon, wout, n2, wg, wu, wd, pn, wpg, wpp, seq):
    n = x1.shape[0]
    tile = TOKEN_TILE
    tiles_per_seq = seq // tile
    n_tiles = n // tile
    halo_u = tile // POOL_HALO
    halo_kv = tile // BLOCK
    res = [pw, ps, pon, aon, wout, n2, wg, wu, wd, pn, wpg, wpp]
    mix = lambda i: jnp.minimum(i, n_tiles - 1)
    tail = lambda i: jnp.maximum(i - 1, 0)
    mix_rows = lambda width: pl.BlockSpec((tile, width), lambda i: (mix(i), 0))
    tail_rows = lambda width: pl.BlockSpec((tile, width), lambda i: (tail(i), 0))
    return pl.pallas_call(
        functools.partial(_stage2_kernel, tiles_per_seq=tiles_per_seq, n_tiles=n_tiles),
        grid=(n_tiles + 1,),
        in_specs=[pl.BlockSpec(memory_space=pltpu.SMEM),
                  mix_rows(D_MODEL),
                  mix_rows(POOL_WIDTH),
                  pl.BlockSpec((POOL_HALO, POOL_WIDTH), lambda i: (jnp.maximum(mix(i) * halo_u - 1, 0), 0)),
                  mix_rows(ATTN_WIDTH),
                  mix_rows(4 * LANES),
                  pl.BlockSpec((BLOCK, 4 * LANES), lambda i: (jnp.maximum(mix(i) * halo_kv - 1, 0), 0)),
                  tail_rows(PLE_DIM)] + [_resident(a.shape) for a in res],
        out_specs=tail_rows(D_MODEL),
        out_shape=jax.ShapeDtypeStruct((n, D_MODEL), F32),
        scratch_shapes=[pltpu.VMEM((tile + POOL_HALO, POOL_WIDTH), F32),
                        pltpu.VMEM((tile, POOL_WIDTH + ATTN_WIDTH), BF16),
                        pltpu.VMEM((tile, POOL_WIDTH), F32),
                        pltpu.VMEM((tile, ATTN_WIDTH), F32),
                        pltpu.VMEM((2, 2 * BLOCK, 2 * BLOCK), F32),
                        pltpu.VMEM((tile, D_MODEL), F32),
                        pltpu.VMEM((tile, D_MODEL), BF16)],
        compiler_params=pltpu.CompilerParams(dimension_semantics=("arbitrary",),
                                             vmem_limit_bytes=VMEM_LIMIT_BYTES),
        name="mix_ffn2_ple",
    )(sinks, x1, u, u, q, kv, kv, p, *res)


def kernel(x, p, ffn1_norm, ffn1_w_gate, ffn1_w_up, ffn1_w_down, mix_norm, w_in, pool_w, pool_scale, q_norm, k_norm, attn_sinks, pool_out_norm, attn_out_norm, w_out, ffn2_norm, ffn2_w_gate, ffn2_w_up, ffn2_w_down, ple_norm, w_ple_gate, w_ple_proj):
    b, s, d = x.shape
    assert d == D_MODEL and s % TOKEN_TILE == 0 and TOKEN_TILE % BLOCK == 0
    depth = p.shape[0]
    xf = x.reshape(b * s, d)
    row = lambda a: a.reshape(1, -1)
    for i in range(depth):
        x1, u, q, kv = _stage1(
            xf, row(ffn1_norm[i]), ffn1_w_gate[i].astype(BF16), ffn1_w_up[i].astype(BF16),
            ffn1_w_down[i].astype(BF16), row(mix_norm[i]), w_in[i].astype(BF16),
            row(jnp.tile(q_norm[i], 2)), row(jnp.tile(k_norm[i], 2)))
        xf = _stage2(
            row(attn_sinks[i]), x1, u, q, kv, p[i].reshape(b * s, PLE_DIM),
            pool_w[i].astype(BF16), row(pool_scale[i]), row(pool_out_norm[i]), row(attn_out_norm[i]),
            w_out[i].astype(BF16), row(ffn2_norm[i]), ffn2_w_gate[i].astype(BF16),
            ffn2_w_up[i].astype(BF16), ffn2_w_down[i].astype(BF16), row(ple_norm[i]),
            w_ple_gate[i].astype(BF16), w_ple_proj[i].astype(BF16), s)
    return xf.reshape(b, s, d)
```

```python
import functools
import math

import jax
import jax.numpy as jnp
from jax import lax
from jax.experimental import pallas as pl
from jax.experimental.pallas import tpu as pltpu

D_MODEL = 1024
PLE_DIM = 256
D_FF = 2816
POOL_WIDTH = 512
POOL_WINDOWS = (2, 4, 8, 16)
POOL_GROUP_DIM = 128
HEAD_DIM = 64
N_Q_HEADS = 8
N_KV_HEADS = 2
ATTN_WIDTH = N_Q_HEADS * HEAD_DIM
KV_WIDTH = N_KV_HEADS * HEAD_DIM
WINDOW = 128
BLOCK = 128
MIX_IN = POOL_WIDTH + ATTN_WIDTH + 2 * KV_WIDTH
EPS = 1e-6
NEG_INF = -1e30

LANES = 128
MXU_DIM = 256
POOL_HALO = 16
TOKEN_TILE = 512
FF_CHUNK = 2 * MXU_DIM
VMEM_LIMIT_BYTES = 56 * 1024 * 1024

F32 = jnp.float32
BF16 = jnp.bfloat16


def _ff_chunks():
    chunks, lo = [], 0
    while lo < D_FF:
        sz = min(FF_CHUNK, D_FF - lo)
        chunks.append((lo, sz))
        lo += sz
    return chunks


N_FFN_SLOTS = 2 * len(_ff_chunks()) + 1


def _rms(x, g):
    return x * lax.rsqrt(jnp.mean(x * x, axis=-1, keepdims=True) + EPS) * g


class _Slots:
    def __init__(self, tasks, n_slots, spread=True):
        self.tasks = list(tasks)
        if spread:
            self.starts = [len(tasks) // n_slots + (s < len(tasks) % n_slots) for s in range(n_slots)]
        else:
            self.starts = [1] * len(tasks) + [0] * (n_slots - len(tasks))
        self.pending = []

    def __call__(self):
        self.flush()
        for _ in range(self.starts.pop(0) if self.starts else 0):
            gen = self.tasks.pop(0)()
            next(gen)
            self.pending.append(gen)

    def flush(self):
        while self.pending:
            for _ in self.pending.pop(0):
                pass

    def finish(self):
        self.flush()
        assert not self.tasks and not self.starts


def _swiglu_chunks(h, wg_ref, wu_ref, wd_ref, slot, cover_last_act=None):
    chunks = _ff_chunks()
    acts = []
    acc = None
    for c, (lo, sz) in enumerate(chunks):
        gate = jnp.dot(h, wg_ref[:, lo:lo + sz], preferred_element_type=F32)
        slot()
        up = jnp.dot(h, wu_ref[:, lo:lo + sz], preferred_element_type=F32)
        slot()
        if cover_last_act is None and c > 0:
            plo, psz = chunks[c - 1]
            part = jnp.dot(acts[-1], wd_ref[plo:plo + psz, :], preferred_element_type=F32)
            slot()
            acc = part if acc is None else acc + part
        acts.append((gate * jax.nn.sigmoid(gate) * up).astype(BF16))
    if cover_last_act is None:
        lo, sz = chunks[-1]
        acc = acc + jnp.dot(acts[-1], wd_ref[lo:lo + sz, :], preferred_element_type=F32)
    else:
        cover_last_act()
        acc = jnp.dot(jnp.concatenate(acts, axis=1), wd_ref[...], preferred_element_type=F32)
    slot()
    return acc


def _head_pair_norm(zb, gain2, low):
    sq = zb * zb
    ms_lo = jnp.sum(jnp.where(low, sq, 0.0), axis=-1, keepdims=True) * (1.0 / HEAD_DIM)
    ms_hi = jnp.sum(jnp.where(low, 0.0, sq), axis=-1, keepdims=True) * (1.0 / HEAD_DIM)
    r = jnp.where(low, lax.rsqrt(ms_lo + EPS), lax.rsqrt(ms_hi + EPS))
    return zb * r * gain2


def _dup_halves(a, low):
    rot = pltpu.roll(a, HEAD_DIM, axis=1)
    return jnp.where(low, a, rot), jnp.where(low, rot, a)


def _mix_in_tasks(x1_scr, nm_ref, win_ref, qg_ref, kg_ref, u_ref, q_ref, kv_ref):
    state = {}
    o_k = POOL_WIDTH + ATTN_WIDTH
    scale = 1.0 / math.sqrt(HEAD_DIM)

    def low():
        return lax.broadcasted_iota(jnp.int32, (1, LANES), 1) < HEAD_DIM

    def project():
        h = _rms(x1_scr[...], nm_ref[...]).astype(BF16)
        yield
        state["z"] = jnp.dot(h, win_ref[...], preferred_element_type=F32)

    def pool_in():
        u_ref[...] = state["z"][:, :POOL_WIDTH]
        yield

    def q_block(j):
        lo = POOL_WIDTH + j * LANES
        qn = _head_pair_norm(state["z"][:, lo:lo + LANES], qg_ref[...], low()) * scale
        q_ref[:, j * LANES:(j + 1) * LANES] = qn.astype(BF16)
        yield

    def k_block():
        kn = _head_pair_norm(state["z"][:, o_k:o_k + KV_WIDTH], kg_ref[...], low())
        for j, a in enumerate(_dup_halves(kn, low())):
            kv_ref[:, j * LANES:(j + 1) * LANES] = a.astype(BF16)
        yield

    def v_block():
        for j, a in enumerate(_dup_halves(state["z"][:, o_k + KV_WIDTH:o_k + 2 * KV_WIDTH], low())):
            kv_ref[:, (N_KV_HEADS + j) * LANES:(N_KV_HEADS + j + 1) * LANES] = a.astype(BF16)
        yield

    return ([project, pool_in] + [functools.partial(q_block, j) for j in range(ATTN_WIDTH // LANES)]
            + [k_block, v_block])


def _stage1_kernel(x_ref, n1_ref, wg_ref, wu_ref, wd_ref, nm_ref, win_ref, qg_ref, kg_ref,
                   x1_ref, u_ref, q_ref, kv_ref, x1_scr):
    i = pl.program_id(0)

    @pl.when(i == 0)
    def _():
        x1_scr[...] = jnp.zeros_like(x1_scr)

    tasks = _mix_in_tasks(x1_scr, nm_ref, win_ref, qg_ref, kg_ref, u_ref, q_ref, kv_ref)
    slots = _Slots(tasks, N_FFN_SLOTS, spread=False)
    x = x_ref[...]
    h = _rms(x, n1_ref[...]).astype(BF16)
    acc = _swiglu_chunks(h, wg_ref, wu_ref, wd_ref, slots)
    slots.finish()
    x1 = x + 0.5 * acc
    x1_ref[...] = x1
    x1_scr[...] = x1


def _mixer_tasks(first, seq_tile, sinks_ref, u_ref, uh_ref, q_ref, kv_ref, kvh_ref,
                 pw_ref, ps_ref, pon_ref, aon_ref, uext_ref, pool_ref, attn_ref, bias_ref, merged_ref):
    tile = u_ref.shape[0]
    pool_tasks, attn_tasks = [], []

    def pool_group(g, win):
        cols = slice(g * POOL_GROUP_DIM, (g + 1) * POOL_GROUP_DIM)
        if g == 0:
            uext_ref[0:POOL_HALO, :] = jnp.where(first, 0.0, uh_ref[...])
            uext_ref[POOL_HALO:, :] = u_ref[...]
            keys = lax.broadcasted_iota(jnp.int32, (2 * BLOCK, 2 * BLOCK), 1)
            bias_ref[1] = jnp.where(keys >= jnp.where(first, BLOCK, 0), bias_ref[0], NEG_INF)
        pos1 = seq_tile * tile + lax.broadcasted_iota(jnp.int32, (tile, 1), 0) + 1
        ext = uext_ref[:, cols]
        s, shift = ext, 1
        while shift < win:
            s = s + pltpu.roll(s, shift, axis=0)
            shift *= 2
        s, tok = s[POOL_HALO:], ext[POOL_HALO:]
        inv_cnt = 1.0 / jnp.minimum(pos1, win).astype(F32)
        d = (s * inv_cnt - tok).astype(BF16)
        yield
        y = jnp.dot(d, pw_ref[g], preferred_element_type=F32)
        pool_ref[:, cols] = y * ps_ref[:, cols]

    for g, win in enumerate(POOL_WINDOWS):
        pool_tasks.append(functools.partial(pool_group, g, win))

    def attn_unit(qb, j):
        low = lax.broadcasted_iota(jnp.int32, (1, LANES), 1) < HEAD_DIM
        top = lax.broadcasted_iota(jnp.int32, (2 * BLOCK, 1), 0) < BLOCK
        zero_bf = jnp.zeros((), BF16)
        r0 = qb * BLOCK
        if qb == 0:
            kv_band = jnp.concatenate([kvh_ref[...], kv_ref[0:BLOCK, :]], axis=0)
        else:
            kv_band = kv_ref[r0 - BLOCK:r0 + BLOCK, :]
        kvh = j // (ATTN_WIDTH // LANES // N_KV_HEADS)
        kd = kv_band[:, kvh * LANES:(kvh + 1) * LANES]
        vd = kv_band[:, (N_KV_HEADS + kvh) * LANES:(N_KV_HEADS + kvh + 1) * LANES]
        qblk = q_ref[r0:r0 + BLOCK, j * LANES:(j + 1) * LANES]
        qq = jnp.concatenate([jnp.where(low, qblk, zero_bf), jnp.where(low, zero_bf, qblk)], axis=0)
        s = lax.dot_general(qq, kd, (((1,), (1,)), ((), ())), preferred_element_type=F32)
        s = s + bias_ref[1 if qb == 0 else 0]
        sink = jnp.where(top, sinks_ref[0, 2 * j], sinks_ref[0, 2 * j + 1])
        m = jnp.maximum(jnp.max(s, axis=-1, keepdims=True), sink)
        e = jnp.exp(s - m).astype(BF16)
        sink_e = jnp.exp(sink - m)
        yield
        v_ones = jnp.concatenate([vd, jnp.ones((2 * BLOCK, LANES), BF16)], axis=1)
        o = jnp.dot(e, v_ones, preferred_element_type=F32)
        o = o[:, :LANES] * (1.0 / (o[:, LANES:] + sink_e))
        attn_ref[r0:r0 + BLOCK, j * LANES:(j + 1) * LANES] = jnp.where(low, o[:BLOCK], o[BLOCK:])

    for qb in range(tile // BLOCK):
        for j in range(ATTN_WIDTH // LANES):
            attn_tasks.append(functools.partial(attn_unit, qb, j))

    def pool_norm():
        merged_ref[:, :POOL_WIDTH] = _rms(pool_ref[...], pon_ref[...]).astype(BF16)
        yield

    def attn_norm():
        merged_ref[:, POOL_WIDTH:] = _rms(attn_ref[...], aon_ref[...]).astype(BF16)
        yield

    tasks = []
    per_pool = -(-len(attn_tasks) // len(pool_tasks))
    while pool_tasks or attn_tasks:
        if pool_tasks:
            tasks.append(pool_tasks.pop(0))
        tasks.extend(attn_tasks[:per_pool])
        del attn_tasks[:per_pool]
    return tasks + [pool_norm, attn_norm]


def _stage2_kernel(sinks_ref, x1_ref, u_ref, uh_ref, q_ref, kv_ref, kvh_ref, p_ref,
                   pw_ref, ps_ref, pon_ref, aon_ref, wout_ref,
                   n2_ref, wg_ref, wu_ref, wd_ref, pn_ref, wpg_ref, wpp_ref,
                   out_ref, uext_ref, merged_ref, pool_ref, attn_ref, bias_ref, x2_scr, h_scr,
                   *, tiles_per_seq, n_tiles):
    i = pl.program_id(0)

    @pl.when(i == 0)
    def _():
        x2_scr[...] = jnp.zeros_like(x2_scr)
        h_scr[...] = jnp.zeros_like(h_scr)
        rows = lax.broadcasted_iota(jnp.int32, (2 * BLOCK, 2 * BLOCK), 0)
        keys = lax.broadcasted_iota(jnp.int32, (2 * BLOCK, 2 * BLOCK), 1)
        qpos = jnp.where(rows >= BLOCK, rows - BLOCK, rows)
        bias_ref[0] = jnp.where((keys > qpos) & (keys <= qpos + BLOCK), 0.0, NEG_INF)

    seq_tile = lax.rem(jnp.minimum(i, n_tiles - 1), tiles_per_seq)
    tasks = _mixer_tasks(seq_tile == 0, seq_tile, sinks_ref, u_ref, uh_ref, q_ref, kv_ref, kvh_ref,
                         pw_ref, ps_ref, pon_ref, aon_ref, uext_ref, pool_ref, attn_ref, bias_ref, merged_ref)
    slots = _Slots(tasks, N_FFN_SLOTS - 1)
    half = D_MODEL // 2
    proj = []

    def first_proj_half():
        slots.finish()
        proj.append(jnp.dot(merged_ref[...], wout_ref[:, :half], preferred_element_type=F32))

    acc = _swiglu_chunks(h_scr[...], wg_ref, wu_ref, wd_ref, slots, first_proj_half)
    x3 = x2_scr[...] + 0.5 * acc
    proj.append(jnp.dot(merged_ref[...], wout_ref[:, half:], preferred_element_type=F32))
    hg = _rms(x3, pn_ref[...]).astype(BF16)
    gate_a = jnp.dot(hg, wpg_ref[:, :half], preferred_element_type=F32)
    x2 = x1_ref[...] + jnp.concatenate(proj, axis=1)
    x2_scr[...] = x2
    h_scr[...] = _rms(x2, n2_ref[...]).astype(BF16)
    gate_b = jnp.dot(hg, wpg_ref[:, half:], preferred_element_type=F32)
    pb = p_ref[...].astype(BF16)
    sig_a = jax.nn.sigmoid(gate_a)
    emb_a = jnp.dot(pb, wpp_ref[:, :half], preferred_element_type=F32)
    sig_b = jax.nn.sigmoid(gate_b)
    emb_b = jnp.dot(pb, wpp_ref[:, half:], preferred_element_type=F32)
    out_ref[:, :half] = x3[:, :half] + sig_a * emb_a
    out_ref[:, half:] = x3[:, half:] + sig_b * emb_b


def _resident(shape):
    nd = len(shape)
    return pl.BlockSpec(shape, lambda i: (0,) * nd, pipeline_mode=pl.Buffered(1))


def _stage1(x, n1, wg, wu, wd, nm, win, qg2, kg2):
    n = x.shape[0]
    tile = TOKEN_TILE
    n_tiles = n // tile
    cur_rows = lambda width: pl.BlockSpec((tile, width), lambda i: (jnp.minimum(i, n_tiles - 1), 0))
    prev_rows = lambda width: pl.BlockSpec((tile, width), lambda i: (jnp.maximum(i - 1, 0), 0))
    return pl.pallas_call(
        _stage1_kernel,
        grid=(n_tiles + 1,),
        in_specs=[cur_rows(D_MODEL), _resident(n1.shape), _resident(wg.shape), _resident(wu.shape),
                  _resident(wd.shape), _resident(nm.shape), _resident(win.shape),
                  _resident(qg2.shape), _resident(kg2.shape)],
        out_specs=[cur_rows(D_MODEL), prev_rows(POOL_WIDTH), prev_rows(ATTN_WIDTH), prev_rows(4 * LANES)],
        out_shape=[jax.ShapeDtypeStruct((n, D_MODEL), F32), jax.ShapeDtypeStruct((n, POOL_WIDTH), F32),
                   jax.ShapeDtypeStruct((n, ATTN_WIDTH), BF16), jax.ShapeDtypeStruct((n, 4 * LANES), BF16)],
        scratch_shapes=[pltpu.VMEM((tile, D_MODEL), F32)],
        compiler_params=pltpu.CompilerParams(dimension_semantics=("arbitrary",),
                                             vmem_limit_bytes=VMEM_LIMIT_BYTES),
        name="ffn1_mix_in",
    )(x, n1, wg, wu, wd, nm, win, qg2, kg2)


def _stage2(sinks, x1, u, q, kv, p, pw, ps, pon, aon, wout, n2, wg, wu, wd, pn, wpg, wpp, seq):
    n = x1.shape[0]
    tile = TOKEN_TILE
    tiles_per_seq = seq // tile
    n_tiles = n // tile
    halo_u = tile // POOL_HALO
    halo_kv = tile // BLOCK
    res = [pw, ps, pon, aon, wout, n2, wg, wu, wd, pn, wpg, wpp]
    mix = lambda i: jnp.minimum(i, n_tiles - 1)
    tail = lambda i: jnp.maximum(i - 1, 0)
    mix_rows = lambda width: pl.BlockSpec((tile, width), lambda i: (mix(i), 0))
    tail_rows = lambda width: pl.BlockSpec((tile, width), lambda i: (tail(i), 0))
    return pl.pallas_call(
        functools.partial(_stage2_kernel, tiles_per_seq=tiles_per_seq, n_tiles=n_tiles),
        grid=(n_tiles + 1,),
        in_specs=[pl.BlockSpec(memory_space=pltpu.SMEM),
                  mix_rows(D_MODEL),
                  mix_rows(POOL_WIDTH),
                  pl.BlockSpec((POOL_HALO, POOL_WIDTH), lambda i: (jnp.maximum(mix(i) * halo_u - 1, 0), 0)),
                  mix_rows(ATTN_WIDTH),
                  mix_rows(4 * LANES),
                  pl.BlockSpec((BLOCK, 4 * LANES), lambda i: (jnp.maximum(mix(i) * halo_kv - 1, 0), 0)),
                  tail_rows(PLE_DIM)] + [_resident(a.shape) for a in res],
        out_specs=tail_rows(D_MODEL),
        out_shape=jax.ShapeDtypeStruct((n, D_MODEL), F32),
        scratch_shapes=[pltpu.VMEM((tile + POOL_HALO, POOL_WIDTH), F32),
                        pltpu.VMEM((tile, POOL_WIDTH + ATTN_WIDTH), BF16),
                        pltpu.VMEM((tile, POOL_WIDTH), F32),
                        pltpu.VMEM((tile, ATTN_WIDTH), F32),
                        pltpu.VMEM((2, 2 * BLOCK, 2 * BLOCK), F32),
                        pltpu.VMEM((tile, D_MODEL), F32),
                        pltpu.VMEM((tile, D_MODEL), BF16)],
        compiler_params=pltpu.CompilerParams(dimension_semantics=("arbitrary",),
                                             vmem_limit_bytes=VMEM_LIMIT_BYTES),
        name="mix_ffn2_ple",
    )(sinks, x1, u, u, q, kv, kv, p, *res)


def kernel(x, p, ffn1_norm, ffn1_w_gate, ffn1_w_up, ffn1_w_down, mix_norm, w_in, pool_w, pool_scale, q_norm, k_norm, attn_sinks, pool_out_norm, attn_out_norm, w_out, ffn2_norm, ffn2_w_gate, ffn2_w_up, ffn2_w_down, ple_norm, w_ple_gate, w_ple_proj):
    b, s, d = x.shape
    assert d == D_MODEL and s % TOKEN_TILE == 0 and TOKEN_TILE % BLOCK == 0
    depth = p.shape[0]
    xf = x.reshape(b * s, d)
    row = lambda a: a.reshape(1, -1)
    for i in range(depth):
        x1, u, q, kv = _stage1(
            xf, row(ffn1_norm[i]), ffn1_w_gate[i].astype(BF16), ffn1_w_up[i].astype(BF16),
            ffn1_w_down[i].astype(BF16), row(mix_norm[i]), w_in[i].astype(BF16),
            row(jnp.tile(q_norm[i], 2)), row(jnp.tile(k_norm[i], 2)))
        xf = _stage2(
            row(attn_sinks[i]), x1, u, q, kv, p[i].reshape(b * s, PLE_DIM),
            pool_w[i].astype(BF16), row(pool_scale[i]), row(pool_out_norm[i]), row(attn_out_norm[i]),
            w_out[i].astype(BF16), row(ffn2_norm[i]), ffn2_w_gate[i].astype(BF16),
            ffn2_w_up[i].astype(BF16), ffn2_w_down[i].astype(BF16), row(ple_norm[i]),
            w_ple_gate[i].astype(BF16), w_ple_proj[i].astype(BF16), s)
    return xf.reshape(b, s, d)
```

```python
import functools
import math

import jax
import jax.numpy as jnp
from jax import lax
from jax.experimental import pallas as pl
from jax.experimental.pallas import tpu as pltpu

D_MODEL = 1024
PLE_DIM = 256
D_FF = 2816
POOL_WIDTH = 512
POOL_WINDOWS = (2, 4, 8, 16)
POOL_GROUP_DIM = 128
HEAD_DIM = 64
N_Q_HEADS = 8
N_KV_HEADS = 2
ATTN_WIDTH = N_Q_HEADS * HEAD_DIM
KV_WIDTH = N_KV_HEADS * HEAD_DIM
WINDOW = 128
BLOCK = 128
MIX_IN = POOL_WIDTH + ATTN_WIDTH + 2 * KV_WIDTH
EPS = 1e-6
NEG_INF = -1e30

LANES = 128
MXU_DIM = 256
POOL_HALO = 16
TOKEN_TILE = 512
FF_CHUNK = 2 * MXU_DIM
VMEM_LIMIT_BYTES = 56 * 1024 * 1024

F32 = jnp.float32
BF16 = jnp.bfloat16


def _ff_chunks():
    chunks, lo = [], 0
    while lo < D_FF:
        sz = min(FF_CHUNK, D_FF - lo)
        chunks.append((lo, sz))
        lo += sz
    return chunks


N_FFN_SLOTS = 2 * len(_ff_chunks()) + 1


def _rms(x, g):
    return x * lax.rsqrt(jnp.mean(x * x, axis=-1, keepdims=True) + EPS) * g


class _Slots:
    def __init__(self, tasks, n_slots, spread=True):
        self.tasks = list(tasks)
        if spread:
            self.starts = [len(tasks) // n_slots + (s < len(tasks) % n_slots) for s in range(n_slots)]
        else:
            self.starts = [1] * len(tasks) + [0] * (n_slots - len(tasks))
        self.pending = []

    def __call__(self):
        self.flush()
        for _ in range(self.starts.pop(0) if self.starts else 0):
            gen = self.tasks.pop(0)()
            next(gen)
            self.pending.append(gen)

    def flush(self):
        while self.pending:
            for _ in self.pending.pop(0):
                pass

    def finish(self):
        self.flush()
        assert not self.tasks and not self.starts


def _swiglu_chunks(h, wg_ref, wu_ref, wd_ref, slot, cover_last_act=None):
    chunks = _ff_chunks()
    acts = []
    acc = None
    for c, (lo, sz) in enumerate(chunks):
        gate = jnp.dot(h, wg_ref[:, lo:lo + sz], preferred_element_type=F32)
        slot()
        up = jnp.dot(h, wu_ref[:, lo:lo + sz], preferred_element_type=F32)
        slot()
        if cover_last_act is None and c > 0:
            plo, psz = chunks[c - 1]
            part = jnp.dot(acts[-1], wd_ref[plo:plo + psz, :], preferred_element_type=F32)
            slot()
            acc = part if acc is None else acc + part
        acts.append((gate * jax.nn.sigmoid(gate) * up).astype(BF16))
    if cover_last_act is None:
        lo, sz = chunks[-1]
        acc = acc + jnp.dot(acts[-1], wd_ref[lo:lo + sz, :], preferred_element_type=F32)
    else:
        cover_last_act()
        acc = jnp.dot(jnp.concatenate(acts, axis=1), wd_ref[...], preferred_element_type=F32)
    slot()
    return acc


def _head_pair_norm(zb, gain2, low):
    sq = zb * zb
    ms_lo = jnp.sum(jnp.where(low, sq, 0.0), axis=-1, keepdims=True) * (1.0 / HEAD_DIM)
    ms_hi = jnp.sum(jnp.where(low, 0.0, sq), axis=-1, keepdims=True) * (1.0 / HEAD_DIM)
    r = jnp.where(low, lax.rsqrt(ms_lo + EPS), lax.rsqrt(ms_hi + EPS))
    return zb * r * gain2


def _dup_halves(a, low):
    rot = pltpu.roll(a, HEAD_DIM, axis=1)
    return jnp.where(low, a, rot), jnp.where(low, rot, a)


def _mix_in_tasks(x1_scr, nm_ref, win_ref, qg_ref, kg_ref, u_ref, q_ref, kt_ref, v_ref):
    state = {}
    o_k = POOL_WIDTH + ATTN_WIDTH
    scale = 1.0 / math.sqrt(HEAD_DIM)

    def low():
        return lax.broadcasted_iota(jnp.int32, (1, LANES), 1) < HEAD_DIM

    def project():
        h = _rms(x1_scr[...], nm_ref[...]).astype(BF16)
        yield
        state["z"] = jnp.dot(h, win_ref[...], preferred_element_type=F32)

    def pool_in():
        u_ref[...] = state["z"][:, :POOL_WIDTH]
        yield

    def q_block(j):
        lo = POOL_WIDTH + j * LANES
        qn = _head_pair_norm(state["z"][:, lo:lo + LANES], qg_ref[...], low()) * scale
        q_ref[:, j * LANES:(j + 1) * LANES] = qn.astype(BF16)
        yield

    def k_block():
        kn = _head_pair_norm(state["z"][:, o_k:o_k + KV_WIDTH], kg_ref[...], low())
        for j, a in enumerate(_dup_halves(kn, low())):
            kt_ref[j * LANES:(j + 1) * LANES, :] = a.T.astype(BF16)
        yield

    def v_block():
        for j, a in enumerate(_dup_halves(state["z"][:, o_k + KV_WIDTH:o_k + 2 * KV_WIDTH], low())):
            v_ref[:, j * LANES:(j + 1) * LANES] = a.astype(BF16)
        yield

    return ([project, pool_in] + [functools.partial(q_block, j) for j in range(ATTN_WIDTH // LANES)]
            + [k_block, v_block])


def _stage1_kernel(x_ref, n1_ref, wg_ref, wu_ref, wd_ref, nm_ref, win_ref, qg_ref, kg_ref,
                   x1_ref, u_ref, q_ref, kt_ref, v_ref, x1_scr):
    i = pl.program_id(0)

    @pl.when(i == 0)
    def _():
        x1_scr[...] = jnp.zeros_like(x1_scr)

    tasks = _mix_in_tasks(x1_scr, nm_ref, win_ref, qg_ref, kg_ref, u_ref, q_ref, kt_ref, v_ref)
    slots = _Slots(tasks, N_FFN_SLOTS, spread=False)
    x = x_ref[...]
    h = _rms(x, n1_ref[...]).astype(BF16)
    acc = _swiglu_chunks(h, wg_ref, wu_ref, wd_ref, slots)
    slots.finish()
    x1 = x + 0.5 * acc
    x1_ref[...] = x1
    x1_scr[...] = x1


def _mixer_tasks(first, seq_tile, sinks_ref, u_ref, uh_ref, q_ref, kt_ref, kth_ref, v_ref, vh_ref,
                 pw_ref, ps_ref, pon_ref, aon_ref, uext_ref, pool_ref, attn_ref, bias_ref, merged_ref):
    tile = u_ref.shape[0]
    pool_tasks, attn_tasks = [], []

    def pool_group(g, win):
        cols = slice(g * POOL_GROUP_DIM, (g + 1) * POOL_GROUP_DIM)
        if g == 0:
            uext_ref[0:POOL_HALO, :] = jnp.where(first, 0.0, uh_ref[...])
            uext_ref[POOL_HALO:, :] = u_ref[...]
            keys = lax.broadcasted_iota(jnp.int32, (2 * BLOCK, 2 * BLOCK), 1)
            bias_ref[1] = jnp.where(keys >= jnp.where(first, BLOCK, 0), bias_ref[0], NEG_INF)
        pos1 = seq_tile * tile + lax.broadcasted_iota(jnp.int32, (tile, 1), 0) + 1
        ext = uext_ref[:, cols]
        s, shift = ext, 1
        while shift < win:
            s = s + pltpu.roll(s, shift, axis=0)
            shift *= 2
        s, tok = s[POOL_HALO:], ext[POOL_HALO:]
        inv_cnt = 1.0 / jnp.minimum(pos1, win).astype(F32)
        d = (s * inv_cnt - tok).astype(BF16)
        yield
        y = jnp.dot(d, pw_ref[g], preferred_element_type=F32)
        pool_ref[:, cols] = y * ps_ref[:, cols]

    for g, win in enumerate(POOL_WINDOWS):
        pool_tasks.append(functools.partial(pool_group, g, win))

    def attn_unit(qb, j):
        low = lax.broadcasted_iota(jnp.int32, (1, LANES), 1) < HEAD_DIM
        top = lax.broadcasted_iota(jnp.int32, (2 * BLOCK, 1), 0) < BLOCK
        zero_bf = jnp.zeros((), BF16)
        r0 = qb * BLOCK
        kvh = j // (ATTN_WIDTH // LANES // N_KV_HEADS)
        feat = slice(kvh * LANES, (kvh + 1) * LANES)
        if qb == 0:
            kt = jnp.concatenate([kth_ref[feat, :], kt_ref[feat, 0:BLOCK]], axis=1)
            vd = jnp.concatenate([vh_ref[:, feat], v_ref[0:BLOCK, feat]], axis=0)
        else:
            kt = kt_ref[feat, r0 - BLOCK:r0 + BLOCK]
            vd = v_ref[r0 - BLOCK:r0 + BLOCK, feat]
        qblk = q_ref[r0:r0 + BLOCK, j * LANES:(j + 1) * LANES]
        qq = jnp.concatenate([jnp.where(low, qblk, zero_bf), jnp.where(low, zero_bf, qblk)], axis=0)
        s = jnp.dot(qq, kt, preferred_element_type=F32)
        s = s + bias_ref[1 if qb == 0 else 0]
        sink = jnp.where(top, sinks_ref[0, 2 * j], sinks_ref[0, 2 * j + 1])
        m = jnp.maximum(jnp.max(s, axis=-1, keepdims=True), sink)
        e = jnp.exp(s - m).astype(BF16)
        sink_e = jnp.exp(sink - m)
        yield
        v_ones = jnp.concatenate([vd, jnp.ones((2 * BLOCK, LANES), BF16)], axis=1)
        o = jnp.dot(e, v_ones, preferred_element_type=F32)
        o = o[:, :LANES] * (1.0 / (o[:, LANES:] + sink_e))
        attn_ref[r0:r0 + BLOCK, j * LANES:(j + 1) * LANES] = jnp.where(low, o[:BLOCK], o[BLOCK:])

    for qb in range(tile // BLOCK):
        for j in range(ATTN_WIDTH // LANES):
            attn_tasks.append(functools.partial(attn_unit, qb, j))

    def pool_norm():
        merged_ref[:, :POOL_WIDTH] = _rms(pool_ref[...], pon_ref[...]).astype(BF16)
        yield

    def attn_norm():
        merged_ref[:, POOL_WIDTH:] = _rms(attn_ref[...], aon_ref[...]).astype(BF16)
        yield

    tasks = []
    per_pool = -(-len(attn_tasks) // len(pool_tasks))
    while pool_tasks or attn_tasks:
        if pool_tasks:
            tasks.append(pool_tasks.pop(0))
        tasks.extend(attn_tasks[:per_pool])
        del attn_tasks[:per_pool]
    return tasks + [pool_norm, attn_norm]


def _stage2_kernel(sinks_ref, x1_ref, u_ref, uh_ref, q_ref, kt_ref, kth_ref, v_ref, vh_ref, p_ref,
                   pw_ref, ps_ref, pon_ref, aon_ref, wout_ref,
                   n2_ref, wg_ref, wu_ref, wd_ref, pn_ref, wpg_ref, wpp_ref,
                   out_ref, uext_ref, merged_ref, pool_ref, attn_ref, bias_ref, x2_scr, h_scr,
                   *, tiles_per_seq, n_tiles):
    i = pl.program_id(0)

    @pl.when(i == 0)
    def _():
        x2_scr[...] = jnp.zeros_like(x2_scr)
        h_scr[...] = jnp.zeros_like(h_scr)
        rows = lax.broadcasted_iota(jnp.int32, (2 * BLOCK, 2 * BLOCK), 0)
        keys = lax.broadcasted_iota(jnp.int32, (2 * BLOCK, 2 * BLOCK), 1)
        qpos = jnp.where(rows >= BLOCK, rows - BLOCK, rows)
        bias_ref[0] = jnp.where((keys > qpos) & (keys <= qpos + BLOCK), 0.0, NEG_INF)

    seq_tile = lax.rem(jnp.minimum(i, n_tiles - 1), tiles_per_seq)
    tasks = _mixer_tasks(seq_tile == 0, seq_tile, sinks_ref, u_ref, uh_ref, q_ref, kt_ref, kth_ref, v_ref, vh_ref,
                         pw_ref, ps_ref, pon_ref, aon_ref, uext_ref, pool_ref, attn_ref, bias_ref, merged_ref)
    slots = _Slots(tasks, N_FFN_SLOTS - 1)
    half = D_MODEL // 2
    proj = []

    def first_proj_half():
        slots.finish()
        proj.append(jnp.dot(merged_ref[...], wout_ref[:, :half], preferred_element_type=F32))

    acc = _swiglu_chunks(h_scr[...], wg_ref, wu_ref, wd_ref, slots, first_proj_half)
    x3 = x2_scr[...] + 0.5 * acc
    proj.append(jnp.dot(merged_ref[...], wout_ref[:, half:], preferred_element_type=F32))
    hg = _rms(x3, pn_ref[...]).astype(BF16)
    gate_a = jnp.dot(hg, wpg_ref[:, :half], preferred_element_type=F32)
    x2 = x1_ref[...] + jnp.concatenate(proj, axis=1)
    x2_scr[...] = x2
    h_scr[...] = _rms(x2, n2_ref[...]).astype(BF16)
    gate_b = jnp.dot(hg, wpg_ref[:, half:], preferred_element_type=F32)
    pb = p_ref[...].astype(BF16)
    sig_a = jax.nn.sigmoid(gate_a)
    emb_a = jnp.dot(pb, wpp_ref[:, :half], preferred_element_type=F32)
    sig_b = jax.nn.sigmoid(gate_b)
    emb_b = jnp.dot(pb, wpp_ref[:, half:], preferred_element_type=F32)
    out_ref[:, :half] = x3[:, :half] + sig_a * emb_a
    out_ref[:, half:] = x3[:, half:] + sig_b * emb_b


def _resident(shape):
    nd = len(shape)
    return pl.BlockSpec(shape, lambda i: (0,) * nd, pipeline_mode=pl.Buffered(1))


def _stage1(x, n1, wg, wu, wd, nm, win, qg2, kg2):
    n = x.shape[0]
    tile = TOKEN_TILE
    n_tiles = n // tile
    cur_rows = lambda width: pl.BlockSpec((tile, width), lambda i: (jnp.minimum(i, n_tiles - 1), 0))
    prev_rows = lambda width: pl.BlockSpec((tile, width), lambda i: (jnp.maximum(i - 1, 0), 0))
    return pl.pallas_call(
        _stage1_kernel,
        grid=(n_tiles + 1,),
        in_specs=[cur_rows(D_MODEL), _resident(n1.shape), _resident(wg.shape), _resident(wu.shape),
                  _resident(wd.shape), _resident(nm.shape), _resident(win.shape),
                  _resident(qg2.shape), _resident(kg2.shape)],
        out_specs=[cur_rows(D_MODEL), prev_rows(POOL_WIDTH), prev_rows(ATTN_WIDTH),
                   pl.BlockSpec((2 * LANES, tile), lambda i: (0, jnp.maximum(i - 1, 0))), prev_rows(2 * LANES)],
        out_shape=[jax.ShapeDtypeStruct((n, D_MODEL), F32), jax.ShapeDtypeStruct((n, POOL_WIDTH), F32),
                   jax.ShapeDtypeStruct((n, ATTN_WIDTH), BF16), jax.ShapeDtypeStruct((2 * LANES, n), BF16),
                   jax.ShapeDtypeStruct((n, 2 * LANES), BF16)],
        scratch_shapes=[pltpu.VMEM((tile, D_MODEL), F32)],
        compiler_params=pltpu.CompilerParams(dimension_semantics=("arbitrary",),
                                             vmem_limit_bytes=VMEM_LIMIT_BYTES),
        name="ffn1_mix_in",
    )(x, n1, wg, wu, wd, nm, win, qg2, kg2)


def _stage2(sinks, x1, u, q, kt, v, p, pw, ps, pon, aon, wout, n2, wg, wu, wd, pn, wpg, wpp, seq):
    n = x1.shape[0]
    tile = TOKEN_TILE
    tiles_per_seq = seq // tile
    n_tiles = n // tile
    halo_u = tile // POOL_HALO
    halo_kv = tile // BLOCK
    res = [pw, ps, pon, aon, wout, n2, wg, wu, wd, pn, wpg, wpp]
    mix = lambda i: jnp.minimum(i, n_tiles - 1)
    tail = lambda i: jnp.maximum(i - 1, 0)
    mix_rows = lambda width: pl.BlockSpec((tile, width), lambda i: (mix(i), 0))
    tail_rows = lambda width: pl.BlockSpec((tile, width), lambda i: (tail(i), 0))
    return pl.pallas_call(
        functools.partial(_stage2_kernel, tiles_per_seq=tiles_per_seq, n_tiles=n_tiles),
        grid=(n_tiles + 1,),
        in_specs=[pl.BlockSpec(memory_space=pltpu.SMEM),
                  mix_rows(D_MODEL),
                  mix_rows(POOL_WIDTH),
                  pl.BlockSpec((POOL_HALO, POOL_WIDTH), lambda i: (jnp.maximum(mix(i) * halo_u - 1, 0), 0)),
                  mix_rows(ATTN_WIDTH),
                  pl.BlockSpec((2 * LANES, tile), lambda i: (0, mix(i))),
                  pl.BlockSpec((2 * LANES, BLOCK), lambda i: (0, jnp.maximum(mix(i) * halo_kv - 1, 0))),
                  mix_rows(2 * LANES),
                  pl.BlockSpec((BLOCK, 2 * LANES), lambda i: (jnp.maximum(mix(i) * halo_kv - 1, 0), 0)),
                  tail_rows(PLE_DIM)] + [_resident(a.shape) for a in res],
        out_specs=tail_rows(D_MODEL),
        out_shape=jax.ShapeDtypeStruct((n, D_MODEL), F32),
        scratch_shapes=[pltpu.VMEM((tile + POOL_HALO, POOL_WIDTH), F32),
                        pltpu.VMEM((tile, POOL_WIDTH + ATTN_WIDTH), BF16),
                        pltpu.VMEM((tile, POOL_WIDTH), F32),
                        pltpu.VMEM((tile, ATTN_WIDTH), F32),
                        pltpu.VMEM((2, 2 * BLOCK, 2 * BLOCK), F32),
                        pltpu.VMEM((tile, D_MODEL), F32),
                        pltpu.VMEM((tile, D_MODEL), BF16)],
        compiler_params=pltpu.CompilerParams(dimension_semantics=("arbitrary",),
                                             vmem_limit_bytes=VMEM_LIMIT_BYTES),
        name="mix_ffn2_ple",
    )(sinks, x1, u, u, q, kt, kt, v, v, p, *res)


def kernel(x, p, ffn1_norm, ffn1_w_gate, ffn1_w_up, ffn1_w_down, mix_norm, w_in, pool_w, pool_scale, q_norm, k_norm, attn_sinks, pool_out_norm, attn_out_norm, w_out, ffn2_norm, ffn2_w_gate, ffn2_w_up, ffn2_w_down, ple_norm, w_ple_gate, w_ple_proj):
    b, s, d = x.shape
    assert d == D_MODEL and s % TOKEN_TILE == 0 and TOKEN_TILE % BLOCK == 0
    depth = p.shape[0]
    xf = x.reshape(b * s, d)
    row = lambda a: a.reshape(1, -1)
    for i in range(depth):
        x1, u, q, kt, v = _stage1(
            xf, row(ffn1_norm[i]), ffn1_w_gate[i].astype(BF16), ffn1_w_up[i].astype(BF16),
            ffn1_w_down[i].astype(BF16), row(mix_norm[i]), w_in[i].astype(BF16),
            row(jnp.tile(q_norm[i], 2)), row(jnp.tile(k_norm[i], 2)))
        xf = _stage2(
            row(attn_sinks[i]), x1, u, q, kt, v, p[i].reshape(b * s, PLE_DIM),
            pool_w[i].astype(BF16), row(pool_scale[i]), row(pool_out_norm[i]), row(attn_out_norm[i]),
            w_out[i].astype(BF16), row(ffn2_norm[i]), ffn2_w_gate[i].astype(BF16),
            ffn2_w_up[i].astype(BF16), ffn2_w_down[i].astype(BF16), row(ple_norm[i]),
            w_ple_gate[i].astype(BF16), w_ple_proj[i].astype(BF16), s)
    return xf.reshape(b, s, d)
```

```python
import functools
import math

import jax
import jax.numpy as jnp
from jax import lax
from jax.experimental import pallas as pl
from jax.experimental.pallas import tpu as pltpu

D_MODEL = 1024
PLE_DIM = 256
D_FF = 2816
POOL_WIDTH = 512
POOL_WINDOWS = (2, 4, 8, 16)
POOL_GROUP_DIM = 128
HEAD_DIM = 64
N_Q_HEADS = 8
N_KV_HEADS = 2
ATTN_WIDTH = N_Q_HEADS * HEAD_DIM
KV_WIDTH = N_KV_HEADS * HEAD_DIM
WINDOW = 128
BLOCK = 128
MIX_IN = POOL_WIDTH + ATTN_WIDTH + 2 * KV_WIDTH
EPS = 1e-6
NEG_INF = -1e30

LANES = 128
MXU_DIM = 256
POOL_HALO = 16
TOKEN_TILE = 512
FF_CHUNK = 2 * MXU_DIM
WEIGHT_STAGE_ROWS = 256
VMEM_LIMIT_BYTES = 56 * 1024 * 1024

F32 = jnp.float32
BF16 = jnp.bfloat16


def _ff_chunks():
    chunks, lo = [], 0
    while lo < D_FF:
        sz = min(FF_CHUNK, D_FF - lo)
        chunks.append((lo, sz))
        lo += sz
    return chunks


N_FFN_SLOTS = 2 * len(_ff_chunks()) + 1


def _stage_weights_bf16(pairs, stage_ref, sem):
    jobs = []
    for src, dst in pairs:
        rows, width = src.shape
        for r0 in range(0, rows, WEIGHT_STAGE_ROWS):
            nr = min(WEIGHT_STAGE_ROWS, rows - r0)
            jobs.append((src.at[pl.ds(r0, nr), :], dst.at[pl.ds(r0, nr), :], nr, width))

    def copy(k):
        src, _, nr, width = jobs[k]
        return pltpu.make_async_copy(src, stage_ref.at[k % 2, pl.ds(0, nr), pl.ds(0, width)], sem.at[k % 2])

    copy(0).start()
    for k, (_, dst, nr, width) in enumerate(jobs):
        if k + 1 < len(jobs):
            copy(k + 1).start()
        copy(k).wait()
        dst[...] = stage_ref[k % 2, pl.ds(0, nr), pl.ds(0, width)].astype(BF16)


def _rms(x, g):
    return x * lax.rsqrt(jnp.mean(x * x, axis=-1, keepdims=True) + EPS) * g


class _Slots:
    def __init__(self, tasks, n_slots, spread=True):
        self.tasks = list(tasks)
        if spread:
            self.starts = [len(tasks) // n_slots + (s < len(tasks) % n_slots) for s in range(n_slots)]
        else:
            self.starts = [1] * len(tasks) + [0] * (n_slots - len(tasks))
        self.pending = []

    def __call__(self):
        self.flush()
        for _ in range(self.starts.pop(0) if self.starts else 0):
            gen = self.tasks.pop(0)()
            next(gen)
            self.pending.append(gen)

    def flush(self):
        while self.pending:
            for _ in self.pending.pop(0):
                pass

    def finish(self):
        self.flush()
        assert not self.tasks and not self.starts


def _swiglu_chunks(h, wg_ref, wu_ref, wd_ref, slot, cover_last_act=None):
    chunks = _ff_chunks()
    acts = []
    acc = None
    for c, (lo, sz) in enumerate(chunks):
        gate = jnp.dot(h, wg_ref[:, lo:lo + sz], preferred_element_type=F32)
        slot()
        up = jnp.dot(h, wu_ref[:, lo:lo + sz], preferred_element_type=F32)
        slot()
        if cover_last_act is None and c > 0:
            plo, psz = chunks[c - 1]
            part = jnp.dot(acts[-1], wd_ref[plo:plo + psz, :], preferred_element_type=F32)
            slot()
            acc = part if acc is None else acc + part
        acts.append((gate * jax.nn.sigmoid(gate) * up).astype(BF16))
    if cover_last_act is None:
        lo, sz = chunks[-1]
        acc = acc + jnp.dot(acts[-1], wd_ref[lo:lo + sz, :], preferred_element_type=F32)
    else:
        cover_last_act()
        acc = jnp.dot(jnp.concatenate(acts, axis=1), wd_ref[...], preferred_element_type=F32)
    slot()
    return acc


def _head_pair_norm(zb, gain2, low):
    sq = zb * zb
    ms_lo = jnp.sum(jnp.where(low, sq, 0.0), axis=-1, keepdims=True) * (1.0 / HEAD_DIM)
    ms_hi = jnp.sum(jnp.where(low, 0.0, sq), axis=-1, keepdims=True) * (1.0 / HEAD_DIM)
    r = jnp.where(low, lax.rsqrt(ms_lo + EPS), lax.rsqrt(ms_hi + EPS))
    return zb * r * gain2


def _dup_halves(a, low):
    rot = pltpu.roll(a, HEAD_DIM, axis=1)
    return jnp.where(low, a, rot), jnp.where(low, rot, a)


def _mix_in_tasks(x1_scr, nm_ref, win_ref, qg_ref, kg_ref, u_ref, q_ref, kv_ref):
    state = {}
    o_k = POOL_WIDTH + ATTN_WIDTH
    scale = 1.0 / math.sqrt(HEAD_DIM)

    def low():
        return lax.broadcasted_iota(jnp.int32, (1, LANES), 1) < HEAD_DIM

    def project():
        h = _rms(x1_scr[...], nm_ref[...]).astype(BF16)
        yield
        state["z"] = jnp.dot(h, win_ref[...], preferred_element_type=F32)

    def pool_in():
        u_ref[...] = state["z"][:, :POOL_WIDTH]
        yield

    def q_block(j):
        lo = POOL_WIDTH + j * LANES
        qn = _head_pair_norm(state["z"][:, lo:lo + LANES], qg_ref[...], low()) * scale
        q_ref[:, j * LANES:(j + 1) * LANES] = qn.astype(BF16)
        yield

    def k_block():
        kn = _head_pair_norm(state["z"][:, o_k:o_k + KV_WIDTH], kg_ref[...], low())
        for j, a in enumerate(_dup_halves(kn, low())):
            kv_ref[:, j * LANES:(j + 1) * LANES] = a.astype(BF16)
        yield

    def v_block():
        for j, a in enumerate(_dup_halves(state["z"][:, o_k + KV_WIDTH:o_k + 2 * KV_WIDTH], low())):
            kv_ref[:, (N_KV_HEADS + j) * LANES:(N_KV_HEADS + j + 1) * LANES] = a.astype(BF16)
        yield

    return ([project, pool_in] + [functools.partial(q_block, j) for j in range(ATTN_WIDTH // LANES)]
            + [k_block, v_block])


def _stage1_kernel(x_ref, n1_ref, wg_hbm, wu_hbm, wd_hbm, nm_ref, win_hbm, qg_ref, kg_ref,
                   x1_ref, u_ref, q_ref, kv_ref, x1_scr, wg_ref, wu_ref, wd_ref, win_ref, stage_ref, sem):
    i = pl.program_id(0)

    @pl.when(i == 0)
    def _():
        x1_scr[...] = jnp.zeros_like(x1_scr)
        _stage_weights_bf16([(wg_hbm, wg_ref), (wu_hbm, wu_ref), (wd_hbm, wd_ref), (win_hbm, win_ref)],
                            stage_ref, sem)

    tasks = _mix_in_tasks(x1_scr, nm_ref, win_ref, qg_ref, kg_ref, u_ref, q_ref, kv_ref)
    slots = _Slots(tasks, N_FFN_SLOTS, spread=False)
    x = x_ref[...]
    h = _rms(x, n1_ref[...]).astype(BF16)
    acc = _swiglu_chunks(h, wg_ref, wu_ref, wd_ref, slots)
    slots.finish()
    x1 = x + 0.5 * acc
    x1_ref[...] = x1
    x1_scr[...] = x1


def _mixer_tasks(first, seq_tile, sinks_ref, u_ref, uh_ref, q_ref, kv_ref, kvh_ref,
                 pw_ref, ps_ref, pon_ref, aon_ref, uext_ref, pool_ref, attn_ref, bias_ref, merged_ref):
    tile = u_ref.shape[0]
    pool_tasks, attn_tasks = [], []

    def pool_group(g, win):
        cols = slice(g * POOL_GROUP_DIM, (g + 1) * POOL_GROUP_DIM)
        if g == 0:
            uext_ref[0:POOL_HALO, :] = jnp.where(first, 0.0, uh_ref[...])
            uext_ref[POOL_HALO:, :] = u_ref[...]
            keys = lax.broadcasted_iota(jnp.int32, (2 * BLOCK, 2 * BLOCK), 1)
            bias_ref[1] = jnp.where(keys >= jnp.where(first, BLOCK, 0), bias_ref[0], NEG_INF)
        pos1 = seq_tile * tile + lax.broadcasted_iota(jnp.int32, (tile, 1), 0) + 1
        ext = uext_ref[:, cols]
        s, shift = ext, 1
        while shift < win:
            s = s + pltpu.roll(s, shift, axis=0)
            shift *= 2
        s, tok = s[POOL_HALO:], ext[POOL_HALO:]
        inv_cnt = 1.0 / jnp.minimum(pos1, win).astype(F32)
        d = (s * inv_cnt - tok).astype(BF16)
        yield
        y = jnp.dot(d, pw_ref[g], preferred_element_type=F32)
        pool_ref[:, cols] = y * ps_ref[:, cols]

    for g, win in enumerate(POOL_WINDOWS):
        pool_tasks.append(functools.partial(pool_group, g, win))

    def attn_unit(qb, j):
        low = lax.broadcasted_iota(jnp.int32, (1, LANES), 1) < HEAD_DIM
        top = lax.broadcasted_iota(jnp.int32, (2 * BLOCK, 1), 0) < BLOCK
        zero_bf = jnp.zeros((), BF16)
        r0 = qb * BLOCK
        if qb == 0:
            kv_band = jnp.concatenate([kvh_ref[...], kv_ref[0:BLOCK, :]], axis=0)
        else:
            kv_band = kv_ref[r0 - BLOCK:r0 + BLOCK, :]
        kvh = j // (ATTN_WIDTH // LANES // N_KV_HEADS)
        kd = kv_band[:, kvh * LANES:(kvh + 1) * LANES]
        vd = kv_band[:, (N_KV_HEADS + kvh) * LANES:(N_KV_HEADS + kvh + 1) * LANES]
        qblk = q_ref[r0:r0 + BLOCK, j * LANES:(j + 1) * LANES]
        qq = jnp.concatenate([jnp.where(low, qblk, zero_bf), jnp.where(low, zero_bf, qblk)], axis=0)
        s = lax.dot_general(qq, kd, (((1,), (1,)), ((), ())), preferred_element_type=F32)
        s = s + bias_ref[1 if qb == 0 else 0]
        sink = jnp.where(top, sinks_ref[0, 2 * j], sinks_ref[0, 2 * j + 1])
        m = jnp.maximum(jnp.max(s, axis=-1, keepdims=True), sink)
        e = jnp.exp(s - m).astype(BF16)
        sink_e = jnp.exp(sink - m)
        yield
        v_ones = jnp.concatenate([vd, jnp.ones((2 * BLOCK, LANES), BF16)], axis=1)
        o = jnp.dot(e, v_ones, preferred_element_type=F32)
        o = o[:, :LANES] * (1.0 / (o[:, LANES:] + sink_e))
        attn_ref[r0:r0 + BLOCK, j * LANES:(j + 1) * LANES] = jnp.where(low, o[:BLOCK], o[BLOCK:])

    for qb in range(tile // BLOCK):
        for j in range(ATTN_WIDTH // LANES):
            attn_tasks.append(functools.partial(attn_unit, qb, j))

    def pool_norm():
        merged_ref[:, :POOL_WIDTH] = _rms(pool_ref[...], pon_ref[...]).astype(BF16)
        yield

    def attn_norm():
        merged_ref[:, POOL_WIDTH:] = _rms(attn_ref[...], aon_ref[...]).astype(BF16)
        yield

    tasks = []
    per_pool = -(-len(attn_tasks) // len(pool_tasks))
    while pool_tasks or attn_tasks:
        if pool_tasks:
            tasks.append(pool_tasks.pop(0))
        tasks.extend(attn_tasks[:per_pool])
        del attn_tasks[:per_pool]
    return tasks + [pool_norm, attn_norm]


def _stage2_kernel(sinks_ref, x1_ref, u_ref, uh_ref, q_ref, kv_ref, kvh_ref, p_ref,
                   pw_hbm, ps_ref, pon_ref, aon_ref, wout_hbm,
                   n2_ref, wg_hbm, wu_hbm, wd_hbm, pn_ref, wpg_hbm, wpp_hbm,
                   out_ref, uext_ref, merged_ref, pool_ref, attn_ref, bias_ref, x2_scr, h_scr,
                   pw_ref, wout_ref, wg_ref, wu_ref, wd_ref, wpg_ref, wpp_ref, stage_ref, sem,
                   *, tiles_per_seq, n_tiles):
    i = pl.program_id(0)

    @pl.when(i == 0)
    def _():
        x2_scr[...] = jnp.zeros_like(x2_scr)
        h_scr[...] = jnp.zeros_like(h_scr)
        rows = lax.broadcasted_iota(jnp.int32, (2 * BLOCK, 2 * BLOCK), 0)
        keys = lax.broadcasted_iota(jnp.int32, (2 * BLOCK, 2 * BLOCK), 1)
        qpos = jnp.where(rows >= BLOCK, rows - BLOCK, rows)
        bias_ref[0] = jnp.where((keys > qpos) & (keys <= qpos + BLOCK), 0.0, NEG_INF)
        pairs = [(pw_hbm.at[g], pw_ref.at[g]) for g in range(len(POOL_WINDOWS))]
        pairs += [(wout_hbm, wout_ref), (wg_hbm, wg_ref), (wu_hbm, wu_ref), (wd_hbm, wd_ref),
                  (wpg_hbm, wpg_ref), (wpp_hbm, wpp_ref)]
        _stage_weights_bf16(pairs, stage_ref, sem)

    seq_tile = lax.rem(jnp.minimum(i, n_tiles - 1), tiles_per_seq)
    tasks = _mixer_tasks(seq_tile == 0, seq_tile, sinks_ref, u_ref, uh_ref, q_ref, kv_ref, kvh_ref,
                         pw_ref, ps_ref, pon_ref, aon_ref, uext_ref, pool_ref, attn_ref, bias_ref, merged_ref)
    slots = _Slots(tasks, N_FFN_SLOTS - 1)
    half = D_MODEL // 2
    proj = []

    def first_proj_half():
        slots.finish()
        proj.append(jnp.dot(merged_ref[...], wout_ref[:, :half], preferred_element_type=F32))

    acc = _swiglu_chunks(h_scr[...], wg_ref, wu_ref, wd_ref, slots, first_proj_half)
    x3 = x2_scr[...] + 0.5 * acc
    proj.append(jnp.dot(merged_ref[...], wout_ref[:, half:], preferred_element_type=F32))
    hg = _rms(x3, pn_ref[...]).astype(BF16)
    gate_a = jnp.dot(hg, wpg_ref[:, :half], preferred_element_type=F32)
    x2 = x1_ref[...] + jnp.concatenate(proj, axis=1)
    x2_scr[...] = x2
    h_scr[...] = _rms(x2, n2_ref[...]).astype(BF16)
    gate_b = jnp.dot(hg, wpg_ref[:, half:], preferred_element_type=F32)
    pb = p_ref[...].astype(BF16)
    sig_a = jax.nn.sigmoid(gate_a)
    emb_a = jnp.dot(pb, wpp_ref[:, :half], preferred_element_type=F32)
    sig_b = jax.nn.sigmoid(gate_b)
    emb_b = jnp.dot(pb, wpp_ref[:, half:], preferred_element_type=F32)
    out_ref[:, :half] = x3[:, :half] + sig_a * emb_a
    out_ref[:, half:] = x3[:, half:] + sig_b * emb_b


def _resident(shape):
    nd = len(shape)
    return pl.BlockSpec(shape, lambda i: (0,) * nd, pipeline_mode=pl.Buffered(1))


_IN_HBM = pl.BlockSpec(memory_space=pl.ANY)


def _weight_scratch(weights):
    width = max(w.shape[-1] for w in weights)
    return ([pltpu.VMEM(w.shape, BF16) for w in weights]
            + [pltpu.VMEM((2, WEIGHT_STAGE_ROWS, width), F32), pltpu.SemaphoreType.DMA((2,))])


def _stage1(x, n1, wg, wu, wd, nm, win, qg2, kg2):
    n = x.shape[0]
    tile = TOKEN_TILE
    n_tiles = n // tile
    cur_rows = lambda width: pl.BlockSpec((tile, width), lambda i: (jnp.minimum(i, n_tiles - 1), 0))
    prev_rows = lambda width: pl.BlockSpec((tile, width), lambda i: (jnp.maximum(i - 1, 0), 0))
    return pl.pallas_call(
        _stage1_kernel,
        grid=(n_tiles + 1,),
        in_specs=[cur_rows(D_MODEL), _resident(n1.shape), _IN_HBM, _IN_HBM, _IN_HBM, _resident(nm.shape), _IN_HBM,
                  _resident(qg2.shape), _resident(kg2.shape)],
        out_specs=[cur_rows(D_MODEL), prev_rows(POOL_WIDTH), prev_rows(ATTN_WIDTH), prev_rows(4 * LANES)],
        out_shape=[jax.ShapeDtypeStruct((n, D_MODEL), F32), jax.ShapeDtypeStruct((n, POOL_WIDTH), F32),
                   jax.ShapeDtypeStruct((n, ATTN_WIDTH), BF16), jax.ShapeDtypeStruct((n, 4 * LANES), BF16)],
        scratch_shapes=[pltpu.VMEM((tile, D_MODEL), F32)] + _weight_scratch([wg, wu, wd, win]),
        compiler_params=pltpu.CompilerParams(dimension_semantics=("arbitrary",),
                                             vmem_limit_bytes=VMEM_LIMIT_BYTES),
        name="ffn1_mix_in",
    )(x, n1, wg, wu, wd, nm, win, qg2, kg2)


def _stage2(sinks, x1, u, q, kv, p, pw, ps, pon, aon, wout, n2, wg, wu, wd, pn, wpg, wpp, seq):
    n = x1.shape[0]
    tile = TOKEN_TILE
    tiles_per_seq = seq // tile
    n_tiles = n // tile
    halo_u = tile // POOL_HALO
    halo_kv = tile // BLOCK
    res = [pw, ps, pon, aon, wout, n2, wg, wu, wd, pn, wpg, wpp]
    weights = [pw, wout, wg, wu, wd, wpg, wpp]
    res_specs = [_IN_HBM if any(a is w for w in weights) else _resident(a.shape) for a in res]
    mix = lambda i: jnp.minimum(i, n_tiles - 1)
    tail = lambda i: jnp.maximum(i - 1, 0)
    mix_rows = lambda width: pl.BlockSpec((tile, width), lambda i: (mix(i), 0))
    tail_rows = lambda width: pl.BlockSpec((tile, width), lambda i: (tail(i), 0))
    return pl.pallas_call(
        functools.partial(_stage2_kernel, tiles_per_seq=tiles_per_seq, n_tiles=n_tiles),
        grid=(n_tiles + 1,),
        in_specs=[pl.BlockSpec(memory_space=pltpu.SMEM),
                  mix_rows(D_MODEL),
                  mix_rows(POOL_WIDTH),
                  pl.BlockSpec((POOL_HALO, POOL_WIDTH), lambda i: (jnp.maximum(mix(i) * halo_u - 1, 0), 0)),
                  mix_rows(ATTN_WIDTH),
                  mix_rows(4 * LANES),
                  pl.BlockSpec((BLOCK, 4 * LANES), lambda i: (jnp.maximum(mix(i) * halo_kv - 1, 0), 0)),
                  tail_rows(PLE_DIM)] + res_specs,
        out_specs=tail_rows(D_MODEL),
        out_shape=jax.ShapeDtypeStruct((n, D_MODEL), F32),
        scratch_shapes=[pltpu.VMEM((tile + POOL_HALO, POOL_WIDTH), F32),
                        pltpu.VMEM((tile, POOL_WIDTH + ATTN_WIDTH), BF16),
                        pltpu.VMEM((tile, POOL_WIDTH), F32),
                        pltpu.VMEM((tile, ATTN_WIDTH), F32),
                        pltpu.VMEM((2, 2 * BLOCK, 2 * BLOCK), F32),
                        pltpu.VMEM((tile, D_MODEL), F32),
                        pltpu.VMEM((tile, D_MODEL), BF16)] + _weight_scratch(weights),
        compiler_params=pltpu.CompilerParams(dimension_semantics=("arbitrary",),
                                             vmem_limit_bytes=VMEM_LIMIT_BYTES),
        name="mix_ffn2_ple",
    )(sinks, x1, u, u, q, kv, kv, p, *res)


def kernel(x, p, ffn1_norm, ffn1_w_gate, ffn1_w_up, ffn1_w_down, mix_norm, w_in, pool_w, pool_scale, q_norm, k_norm, attn_sinks, pool_out_norm, attn_out_norm, w_out, ffn2_norm, ffn2_w_gate, ffn2_w_up, ffn2_w_down, ple_norm, w_ple_gate, w_ple_proj):
    b, s, d = x.shape
    assert d == D_MODEL and s % TOKEN_TILE == 0 and TOKEN_TILE % BLOCK == 0
    depth = p.shape[0]
    xf = x.reshape(b * s, d)
    row = lambda a: a.reshape(1, -1)
    for i in range(depth):
        x1, u, q, kv = _stage1(
            xf, row(ffn1_norm[i]), ffn1_w_gate[i], ffn1_w_up[i], ffn1_w_down[i], row(mix_norm[i]), w_in[i],
            row(jnp.tile(q_norm[i], 2)), row(jnp.tile(k_norm[i], 2)))
        xf = _stage2(
            row(attn_sinks[i]), x1, u, q, kv, p[i].reshape(b * s, PLE_DIM),
            pool_w[i], row(pool_scale[i]), row(pool_out_norm[i]), row(attn_out_norm[i]),
            w_out[i], row(ffn2_norm[i]), ffn2_w_gate[i], ffn2_w_up[i], ffn2_w_down[i], row(ple_norm[i]),
            w_ple_gate[i], w_ple_proj[i], s)
    return xf.reshape(b, s, d)
```
